```python
import math
import jax, jax.numpy as jnp
from jax import lax
import numpy as np

D_MODEL = 1024
BATCH = 32
SEQ = 2048
DEPTH = 2
DEC_BATCH = 1
DEC_SEQ = 16384
PAST_LEN = 128

GRID_W = 64
BLOCK_Q = 128
N_MIXERS = 2
N_A_LAYERS = (DEPTH + 1) // 2
N_B_LAYERS = DEPTH // 2

A_HEADS = 16
A_KV_HEADS = 4
A_GROUP = A_HEADS // A_KV_HEADS
A_HEAD_DIM = D_MODEL // A_HEADS
A_Q_W = A_HEADS * A_HEAD_DIM
A_KV_W = A_KV_HEADS * A_HEAD_DIM
ROPE_AXIS_DIM = A_HEAD_DIM // 2
ROPE_THETA = 10000.0

B_HEADS = 8
B_HEAD_DIM = D_MODEL // (2 * B_HEADS)
B_V_DIM = 2 * B_HEAD_DIM
B_QK_W = B_HEADS * 2 * B_HEAD_DIM
B_V_W = B_HEADS * B_V_DIM

NUM_BUCKETS = 32
MAX_DISTANCE = 128

D_FF = int(math.ceil(8 * D_MODEL / 3 / 256) * 256)
N_MOD = 6
EPS = 1e-6

kernel_name = "hybrid_gqa_axialrope_diffattn_t5bias_adaln_encoder"


def rms_norm(x, gain):
    x32 = x.astype(jnp.float32)
    y = x32 * lax.rsqrt(jnp.mean(x32 * x32, axis=-1, keepdims=True) + EPS)
    return (y * gain.astype(jnp.float32)).astype(x.dtype)


def axial_rope(length):
    n_rows = length // GRID_W
    rr, cc = jnp.meshgrid(jnp.arange(n_rows), jnp.arange(GRID_W), indexing="ij")
    rows = rr.reshape(-1).astype(jnp.float32)
    cols = cc.reshape(-1).astype(jnp.float32)
    n_pairs = ROPE_AXIS_DIM // 2
    inv_freq = ROPE_THETA ** (-jnp.arange(n_pairs, dtype=jnp.float32) / n_pairs)
    ang = jnp.concatenate([rows[:, None] * inv_freq[None], cols[:, None] * inv_freq[None]], axis=-1)
    return jnp.cos(ang), jnp.sin(ang)


def apply_rope(x, cos, sin):
    x32 = x.astype(jnp.float32).reshape(x.shape[:-1] + (x.shape[-1] // 2, 2))
    x0, x1 = x32[..., 0], x32[..., 1]
    c = cos[None, :, None, :]
    s = sin[None, :, None, :]
    out = jnp.stack([x0 * c - x1 * s, x0 * s + x1 * c], axis=-1)
    return out.reshape(x.shape).astype(x.dtype)


def t5_bucket(rel):
    nb = NUM_BUCKETS // 2
    max_exact = nb // 2
    ret = (rel > 0).astype(jnp.int32) * nb
    n = jnp.abs(rel)
    large = max_exact + (jnp.log(jnp.maximum(n, 1).astype(jnp.float32) / max_exact)
                         / math.log(MAX_DISTANCE / max_exact) * (nb - max_exact)).astype(jnp.int32)
    large = jnp.minimum(large, nb - 1)
    return ret + jnp.where(n < max_exact, n, large)


def gqa_mixer(h, w_qkv, w_o, q_gain, k_gain):
    B, L, _ = h.shape
    qkv = h @ w_qkv
    q, k, v = jnp.split(qkv, [A_Q_W, A_Q_W + A_KV_W], axis=-1)
    q = rms_norm(q.reshape(B, L, A_HEADS, A_HEAD_DIM), q_gain)
    k = rms_norm(k.reshape(B, L, A_KV_HEADS, A_HEAD_DIM), k_gain)
    v = v.reshape(B, L, A_KV_HEADS, A_HEAD_DIM)
    cos, sin = axial_rope(L)
    q = apply_rope(q, cos, sin)
    k = apply_rope(k, cos, sin)
    nb = L // BLOCK_Q
    qb = q.reshape(B, nb, BLOCK_Q, A_KV_HEADS, A_GROUP, A_HEAD_DIM).transpose(1, 0, 2, 3, 4, 5)
    scale = A_HEAD_DIM ** -0.5

    def block(q_blk):
        s = jnp.einsum("bqkgd,bskd->bkgqs", q_blk, k).astype(jnp.float32) * scale
        p = jax.nn.softmax(s, axis=-1).astype(v.dtype)
        return jnp.einsum("bkgqs,bskd->bqkgd", p, v)

    out = lax.map(block, qb)
    out = out.transpose(1, 0, 2, 3, 4, 5).reshape(B, L, A_Q_W)
    return out @ w_o


def diff_mixer(h, w_qkv, w_o, q_gain, k_gain, lq1, lk1, lq2, lk2, subln_gain, rel_bias, lambda_init):
    B, L, _ = h.shape
    qkv = h @ w_qkv
    q, k, v = jnp.split(qkv, [B_QK_W, 2 * B_QK_W], axis=-1)
    q = rms_norm(q.reshape(B, L, B_HEADS, 2, B_HEAD_DIM), q_gain)
    k = rms_norm(k.reshape(B, L, B_HEADS, 2, B_HEAD_DIM), k_gain)
    v = v.reshape(B, L, B_HEADS, B_V_DIM)
    lam = (jnp.exp(jnp.sum(lq1.astype(jnp.float32) * lk1.astype(jnp.float32)))
           - jnp.exp(jnp.sum(lq2.astype(jnp.float32) * lk2.astype(jnp.float32))) + lambda_init)
    nb = L // BLOCK_Q
    qb = q.reshape(B, nb, BLOCK_Q, B_HEADS, 2, B_HEAD_DIM).transpose(1, 0, 2, 3, 4, 5)
    starts = jnp.arange(nb, dtype=jnp.int32) * BLOCK_Q
    kpos = jnp.arange(L, dtype=jnp.int32)
    scale = B_HEAD_DIM ** -0.5
    table = rel_bias.astype(jnp.float32)

    def block(args):
        q_blk, start = args
        qpos = start + jnp.arange(BLOCK_Q, dtype=jnp.int32)
        bias = table[t5_bucket(kpos[None, :] - qpos[:, None])]
        bias = jnp.transpose(bias, (2, 0, 1))
        s = jnp.einsum("bqhcd,bshcd->bhcqs", q_blk, k).astype(jnp.float32) * scale + bias[None, :, None]
        p = jax.nn.softmax(s, axis=-1)
        wgt = p[:, :, 0] - lam * p[:, :, 1]
        return jnp.einsum("bhqs,bshe->bqhe", wgt.astype(v.dtype), v)

    out = lax.map(block, (qb, starts))
    out = out.transpose(1, 0, 2, 3, 4).reshape(B, L, B_HEADS, B_V_DIM)
    out = rms_norm(out, subln_gain) * (1.0 - lambda_init)
    return out.reshape(B, L, B_V_W) @ w_o


def swiglu(h, w_gate_up, w_down):
    g, u = jnp.split(h @ w_gate_up, 2, axis=-1)
    return (jax.nn.silu(g) * u) @ w_down


def _trunk(x, c, norm1_gain, norm2_gain, w_ada, b_ada, w_gate_up, w_down, rel_bias,
           a_w_qkv, a_w_o, a_q_gain, a_k_gain,
           b_w_qkv, b_w_o, b_q_gain, b_k_gain, b_lq1, b_lk1, b_lq2, b_lk2, b_subln_gain):
    c_act = jax.nn.silu(c)
    for i in range(DEPTH):
        mod = c_act @ w_ada[i] + b_ada[i]
        sh1, sc1, g1, sh2, sc2, g2 = jnp.split(mod[:, None, :], N_MOD, axis=-1)
        h = rms_norm(x, norm1_gain[i]) * (1 + sc1) + sh1
        if i % N_MIXERS == 0:
            j = i // N_MIXERS
            h = gqa_mixer(h, a_w_qkv[j], a_w_o[j], a_q_gain[j], a_k_gain[j])
        else:
            j = i // N_MIXERS
            lambda_init = 0.8 - 0.6 * math.exp(-0.3 * i)
            h = diff_mixer(h, b_w_qkv[j], b_w_o[j], b_q_gain[j], b_k_gain[j],
                           b_lq1[j], b_lk1[j], b_lq2[j], b_lk2[j], b_subln_gain[j], rel_bias, lambda_init)
        x = x + g1 * h
        h = rms_norm(x, norm2_gain[i]) * (1 + sc2) + sh2
        x = x + g2 * swiglu(h, w_gate_up[i], w_down[i])
    return x


def setup_inputs(seed: int = 0) -> dict:
    key = jax.random.key(seed)
    ks = jax.random.split(key, 32)
    f32 = jnp.float32
    n = lambda k, shape, s: jax.random.normal(k, shape, f32) * s
    D = D_MODEL
    return {
        "x_prompt": n(ks[0], (BATCH, SEQ, D), 1.0),
        "x_sample": n(ks[1], (DEC_BATCH, DEC_SEQ, D), 1.0),
        "c_prompt": n(ks[2], (BATCH, D), 1.0),
        "c_sample": n(ks[3], (DEC_BATCH, D), 1.0),
        "norm1_gain": 1.0 + n(ks[4], (DEPTH, D), 0.02),
        "norm2_gain": 1.0 + n(ks[5], (DEPTH, D), 0.02),
        "w_ada": n(ks[6], (DEPTH, D, N_MOD * D), 0.5 * D ** -0.5),
        "b_ada": n(ks[7], (DEPTH, N_MOD * D), 0.02),
        "w_gate_up": n(ks[8], (DEPTH, D, 2 * D_FF), D ** -0.5),
        "w_down": n(ks[9], (DEPTH, D_FF, D), D_FF ** -0.5),
        "rel_bias": n(ks[10], (NUM_BUCKETS, B_HEADS), 0.5),
        "a_w_qkv": n(ks[11], (N_A_LAYERS, D, A_Q_W + 2 * A_KV_W), D ** -0.5),
        "a_w_o": n(ks[12], (N_A_LAYERS, A_Q_W, D), A_Q_W ** -0.5),
        "a_q_gain": 1.0 + n(ks[13], (N_A_LAYERS, A_HEAD_DIM), 0.02),
        "a_k_gain": 1.0 + n(ks[14], (N_A_LAYERS, A_HEAD_DIM), 0.02),
        "b_w_qkv": n(ks[15], (N_B_LAYERS, D, 2 * B_QK_W + B_V_W), D ** -0.5),
        "b_w_o": n(ks[16], (N_B_LAYERS, B_V_W, D), B_V_W ** -0.5),
        "b_q_gain": 1.0 + n(ks[17], (N_B_LAYERS, B_HEAD_DIM), 0.02),
        "b_k_gain": 1.0 + n(ks[18], (N_B_LAYERS, B_HEAD_DIM), 0.02),
        "b_lq1": n(ks[19], (N_B_LAYERS, B_HEAD_DIM), 0.1),
        "b_lk1": n(ks[20], (N_B_LAYERS, B_HEAD_DIM), 0.1),
        "b_lq2": n(ks[21], (N_B_LAYERS, B_HEAD_DIM), 0.1),
        "b_lk2": n(ks[22], (N_B_LAYERS, B_HEAD_DIM), 0.1),
        "b_subln_gain": 1.0 + n(ks[23], (N_B_LAYERS, B_V_DIM), 0.02),
    }


def reference(x_prompt, x_sample, c_prompt, c_sample, norm1_gain, norm2_gain, w_ada, b_ada,
              w_gate_up, w_down, rel_bias, a_w_qkv, a_w_o, a_q_gain, a_k_gain,
              b_w_qkv, b_w_o, b_q_gain, b_k_gain, b_lq1, b_lk1, b_lq2, b_lk2, b_subln_gain):
    y_prompt = _trunk(x_prompt, c_prompt, norm1_gain, norm2_gain, w_ada, b_ada, w_gate_up, w_down, rel_bias,
                      a_w_qkv, a_w_o, a_q_gain, a_k_gain,
                      b_w_qkv, b_w_o, b_q_gain, b_k_gain, b_lq1, b_lk1, b_lq2, b_lk2, b_subln_gain)
    y_sample = _trunk(x_sample, c_sample, norm1_gain, norm2_gain, w_ada, b_ada, w_gate_up, w_down, rel_bias,
                      a_w_qkv, a_w_o, a_q_gain, a_k_gain,
                      b_w_qkv, b_w_o, b_q_gain, b_k_gain, b_lq1, b_lk1, b_lq2, b_lk2, b_subln_gain)
    return (y_prompt, y_sample)
```

```python
import functools
import math

import jax
import jax.numpy as jnp
from jax import lax
from jax.experimental import pallas as pl
from jax.experimental.pallas import tpu as pltpu

F32 = jnp.float32
BF16 = jnp.bfloat16

D_MODEL = 1024
DEPTH = 2
GRID_W = 64
HEAD_DIM = 64
A_HEADS = 16
A_KV_HEADS = 4
B_HEADS = 8
B_V_DIM = 128
NUM_BUCKETS = 32
MAX_DISTANCE = 128
ROPE_THETA = 10000.0
D_FF = 2816
N_MOD = 6
EPS = 1e-6

LANES = 128
BF16_ROWS = 16
KV_CHUNK = 256
ONES_ROWS = BF16_ROWS
NEG_BIG = -1e30
VMEM_LIMIT = 56 * 1024 * 1024


def _cparams(sem):
    return pltpu.CompilerParams(dimension_semantics=sem, vmem_limit_bytes=VMEM_LIMIT)


def _mod_kernel(c_ref, w_ref, b_ref, o_ref):
    c = c_ref[...]
    c_act = (c / (1.0 + jnp.exp(-c))).astype(BF16)
    w = w_ref[0].astype(BF16)
    o_ref[0] = jnp.dot(c_act, w, preferred_element_type=F32) + b_ref[0]


def _modulation(c_all, w_ada, b_ada):
    s = c_all.shape[0]
    tn = 1024
    n_out = N_MOD * D_MODEL
    return pl.pallas_call(
        _mod_kernel,
        grid=(DEPTH, n_out // tn),
        in_specs=[
            pl.BlockSpec((s, D_MODEL), lambda i, n: (0, 0)),
            pl.BlockSpec((1, D_MODEL, tn), lambda i, n: (i, 0, n)),
            pl.BlockSpec((1, 1, tn), lambda i, n: (i, 0, n)),
        ],
        out_specs=pl.BlockSpec((1, s, tn), lambda i, n: (i, 0, n)),
        out_shape=jax.ShapeDtypeStruct((DEPTH, s, n_out), F32),
        compiler_params=_cparams(("arbitrary", "arbitrary")),
        name="adaln_mod",
    )(c_all, w_ada, b_ada.reshape(DEPTH, 1, n_out))


def _rms_modulate(x, gain, scale, shift):
    ms = jnp.mean(x * x, axis=-1, keepdims=True)
    y = x * lax.rsqrt(ms + EPS)
    return (y * gain) * (1.0 + scale) + shift


def _pre_kernel(x_ref, mod_ref, g1_ref, wt_ref, gq_ref, gk_ref, *rest, n_q, n_k, n_v, v_dim, rope):
    if rope:
        cos_ref, sin_ref, q_ref, k_ref, v_ref = rest
    else:
        q_ref, k_ref, v_ref = rest
    tm = x_ref.shape[1]
    mod = mod_ref[0, 0]
    h = _rms_modulate(x_ref[0], g1_ref[0], mod[1:2], mod[0:1]).astype(BF16)
    qkv_t = lax.dot_general(wt_ref[...], h, (((1,), (1,)), ((), ())), preferred_element_type=F32)

    def head_norm(blk, gain):
        ms = jnp.mean(blk * blk, axis=0, keepdims=True)
        y = blk * lax.rsqrt(ms + EPS) * gain
        if rope:
            half = HEAD_DIM // 2
            x1, x2 = y[:half], y[half:]
            c, s = cos_ref[...], sin_ref[...]
            y = jnp.concatenate([x1 * c - x2 * s, x1 * s + x2 * c], axis=0)
        return y

    gq, gk = gq_ref[...], gk_ref[...]
    for i in range(n_q):
        blk = head_norm(qkv_t[i * HEAD_DIM:(i + 1) * HEAD_DIM], gq)
        q_ref[0, i * HEAD_DIM:(i + 1) * HEAD_DIM, :] = (blk * (HEAD_DIM ** -0.5)).astype(BF16)
    k_rows = n_q * HEAD_DIM
    k_t = jnp.concatenate(
        [head_norm(qkv_t[k_rows + i * HEAD_DIM:k_rows + (i + 1) * HEAD_DIM], gk) for i in range(n_k)], axis=0)
    k_ref[0] = k_t.T.astype(BF16)
    v_rows = k_rows + n_k * HEAD_DIM
    ones_blk = (lax.broadcasted_iota(jnp.int32, (ONES_ROWS, KV_CHUNK), 0) == 0).astype(BF16)
    for g in range(n_v):
        v_t = qkv_t[v_rows + g * v_dim:v_rows + (g + 1) * v_dim].astype(BF16)
        for c in range(tm // KV_CHUNK):
            v_ref[0, g, c, 0:v_dim, :] = v_t[:, c * KV_CHUNK:(c + 1) * KV_CHUNK]
            v_ref[0, g, c, v_dim:v_dim + ONES_ROWS, :] = ones_blk


def _pre_attention(x, mod, layer, seq_off, g1, wt, gq, gk, cos_t, sin_t, *, n_q, n_k, n_v, v_dim, tm):
    b, l, d = x.shape
    rope = cos_t is not None
    n_rows = wt.shape[0]
    nc = l // KV_CHUNK
    in_specs = [
        pl.BlockSpec((1, tm, d), lambda bi, ti: (bi, ti, 0)),
        pl.BlockSpec((1, 1, N_MOD, d), lambda bi, ti: (layer, bi + seq_off, 0, 0)),
        pl.BlockSpec((1, 1, d), lambda bi, ti: (layer, 0, 0)),
        pl.BlockSpec((n_rows, d), lambda bi, ti: (0, 0)),
        pl.BlockSpec((HEAD_DIM, 1), lambda bi, ti: (0, 0)),
        pl.BlockSpec((HEAD_DIM, 1), lambda bi, ti: (0, 0)),
    ]
    args = [x, mod, g1, wt, gq, gk]
    if rope:
        in_specs += [pl.BlockSpec((HEAD_DIM // 2, tm), lambda bi, ti: (0, ti))] * 2
        args += [cos_t, sin_t]
    vr = v_dim + ONES_ROWS
    out_shape = (
        jax.ShapeDtypeStruct((b, n_q * HEAD_DIM, l), BF16),
        jax.ShapeDtypeStruct((b, l, n_k * HEAD_DIM), BF16),
        jax.ShapeDtypeStruct((b, n_v, nc, vr, KV_CHUNK), BF16),
    )
    out_specs = (
        pl.BlockSpec((1, n_q * HEAD_DIM, tm), lambda bi, ti: (bi, 0, ti)),
        pl.BlockSpec((1, tm, n_k * HEAD_DIM), lambda bi, ti: (bi, ti, 0)),
        pl.BlockSpec((1, n_v, tm // KV_CHUNK, vr, KV_CHUNK), lambda bi, ti: (bi, 0, ti, 0, 0)),
    )
    kern = functools.partial(_pre_kernel, n_q=n_q, n_k=n_k, n_v=n_v, v_dim=v_dim, rope=rope)
    return pl.pallas_call(
        kern, grid=(b, l // tm), in_specs=in_specs, out_specs=out_specs, out_shape=out_shape,
        compiler_params=_cparams(("parallel", "parallel")),
        name="pre_attn_rope" if rope else "pre_attn_diff",
    )(*args)


def _pad_queries(q_ref, qpad_scr, n_vh, half_of):
    tq = q_ref.shape[2]
    zeros = jnp.zeros((HEAD_DIM, tq), BF16)
    for i in range(n_vh):
        q = q_ref[0, i * HEAD_DIM:(i + 1) * HEAD_DIM, :]
        qpad_scr[i] = jnp.concatenate([q, zeros] if half_of(i) == 0 else [zeros, q], axis=0)


def _flash_step(kc, qpad, v_t, m_old, acc_old, bias=None, const=None):
    s = jnp.dot(kc, qpad, preferred_element_type=F32)
    if bias is not None:
        s = s + bias
    m_cur = jnp.max(s, axis=0, keepdims=True)
    if const is not None:
        m_new = jnp.maximum(m_old, m_cur + const)
        p = jnp.exp(s - (m_new - const))
    else:
        m_new = jnp.maximum(m_old, m_cur)
        p = jnp.exp(s - m_new)
    alpha = jnp.exp(m_old - m_new)
    pv = jnp.dot(v_t, p.astype(BF16), preferred_element_type=F32)
    return m_new, alpha * acc_old + pv


def _gqa_attn_kernel(q_ref, k_ref, v_ref, o_ref, qpad_scr, m_scr, acc_scr, *, n_vh, group):
    nc = v_ref.shape[2]
    _pad_queries(q_ref, qpad_scr, n_vh, lambda i: i // group)
    m_scr[...] = jnp.full(m_scr.shape, NEG_BIG, F32)
    acc_scr[...] = jnp.zeros(acc_scr.shape, F32)

    def body(j, carry):
        kc = k_ref[0, pl.ds(pl.multiple_of(j * KV_CHUNK, KV_CHUNK), KV_CHUNK), :]
        for i in range(n_vh):
            m_new, acc_new = _flash_step(kc, qpad_scr[i], v_ref[0, i // group, j], m_scr[i], acc_scr[i])
            m_scr[i] = m_new
            acc_scr[i] = acc_new
        return carry

    lax.fori_loop(0, nc, body, 0)
    for i in range(n_vh):
        acc = acc_scr[i]
        o_ref[0, i * HEAD_DIM:(i + 1) * HEAD_DIM, :] = (acc[:HEAD_DIM] / acc[HEAD_DIM:HEAD_DIM + 1]).astype(BF16)


def _gqa_attention(q_t, k, v_t, *, tq):
    b, _, l = q_t.shape
    group = A_HEADS // A_KV_HEADS
    n_vh = 2 * group
    rows = n_vh * HEAD_DIM
    nc = l // KV_CHUNK
    vr = HEAD_DIM + ONES_ROWS
    kern = functools.partial(_gqa_attn_kernel, n_vh=n_vh, group=group)
    return pl.pallas_call(
        kern,
        grid=(b, A_KV_HEADS // 2, l // tq),
        in_specs=[
            pl.BlockSpec((1, rows, tq), lambda bi, p, qi: (bi, p, qi)),
            pl.BlockSpec((1, l, LANES), lambda bi, p, qi: (bi, 0, p)),
            pl.BlockSpec((1, 2, nc, vr, KV_CHUNK), lambda bi, p, qi: (bi, p, 0, 0, 0)),
        ],
        out_specs=pl.BlockSpec((1, rows, tq), lambda bi, p, qi: (bi, p, qi)),
        out_shape=jax.ShapeDtypeStruct(q_t.shape, BF16),
        scratch_shapes=[
            pltpu.VMEM((n_vh, 2 * HEAD_DIM, tq), BF16),
            pltpu.VMEM((n_vh, 1, tq), F32),
            pltpu.VMEM((n_vh, vr, tq), F32),
        ],
        compiler_params=_cparams(("parallel", "parallel", "arbitrary")),
        name="gqa_attention",
    )(q_t, k, v_t)


def _diff_attn_kernel(tbl_ref, q_ref, k_ref, v_ref, bias_ref, lq1_ref, lk1_ref, lq2_ref, lk2_ref, sg_ref,
                      o_ref, qpad_scr, m_scr, acc_scr, *, lambda_init, n_near):
    nc = v_ref.shape[2]
    tq = q_ref.shape[2]
    h = pl.program_id(1)
    qi = pl.program_id(2)
    _pad_queries(q_ref, qpad_scr, 2, lambda i: i)
    m_scr[...] = jnp.full(m_scr.shape, NEG_BIG, F32)
    acc_scr[...] = jnp.zeros(acc_scr.shape, F32)

    def step(j, bias=None, const=None):
        kc = k_ref[0, pl.ds(pl.multiple_of(j * KV_CHUNK, KV_CHUNK), KV_CHUNK), :]
        v_t = v_ref[0, 0, j]
        for c in range(2):
            m_new, acc_new = _flash_step(kc, qpad_scr[c], v_t, m_scr[c], acc_scr[c], bias=bias, const=const)
            m_scr[c] = m_new
            acc_scr[c] = acc_new

    q_per_k = tq // KV_CHUNK
    near_lo = qi * q_per_k - 1
    left_const = tbl_ref[NUM_BUCKETS // 2 - 1, h]
    right_const = tbl_ref[NUM_BUCKETS - 1, h]

    def far_left(j, carry):
        step(j, const=left_const)
        return carry

    def far_right(j, carry):
        step(j, const=right_const)
        return carry

    lax.fori_loop(0, jnp.maximum(near_lo, 0), far_left, 0)
    for d in range(n_near):
        j = near_lo + d

        @pl.when(jnp.logical_and(j >= 0, j < nc))
        def _():
            step(j, bias=bias_ref[0, d])

    lax.fori_loop(jnp.minimum(near_lo + n_near, nc), nc, far_right, 0)

    lam = (jnp.exp(jnp.sum(lq1_ref[...] * lk1_ref[...], axis=1, keepdims=True))
           - jnp.exp(jnp.sum(lq2_ref[...] * lk2_ref[...], axis=1, keepdims=True)) + lambda_init)
    a0, a1 = acc_scr[0], acc_scr[1]
    o = a0[:B_V_DIM] / a0[B_V_DIM:B_V_DIM + 1] - lam * (a1[:B_V_DIM] / a1[B_V_DIM:B_V_DIM + 1])
    ms = jnp.mean(o * o, axis=0, keepdims=True)
    y = (o * lax.rsqrt(ms + EPS)) * sg_ref[...]
    o_ref[0] = (y * (1.0 - lambda_init)).astype(BF16)


def _diff_attention(q_t, k, v_t, bias_tiles, table, lq1, lk1, lq2, lk2, subln_gain, *, tq, lambda_init):
    b, _, l = q_t.shape
    nc = l // KV_CHUNK
    vr = B_V_DIM + ONES_ROWS
    n_near = bias_tiles.shape[1]
    rows = 2 * HEAD_DIM
    kern = functools.partial(_diff_attn_kernel, lambda_init=lambda_init, n_near=n_near)
    vec = pl.BlockSpec((1, HEAD_DIM), lambda bi, h, qi, *_: (0, 0))
    grid_spec = pltpu.PrefetchScalarGridSpec(
        num_scalar_prefetch=1,
        grid=(b, B_HEADS, l // tq),
        in_specs=[
            pl.BlockSpec((1, rows, tq), lambda bi, h, qi, *_: (bi, h, qi)),
            pl.BlockSpec((1, l, LANES), lambda bi, h, qi, *_: (bi, 0, h)),
            pl.BlockSpec((1, 1, nc, vr, KV_CHUNK), lambda bi, h, qi, *_: (bi, h, 0, 0, 0)),
            pl.BlockSpec((1, n_near, KV_CHUNK, tq), lambda bi, h, qi, *_: (h, 0, 0, 0)),
            vec, vec, vec, vec,
            pl.BlockSpec((B_V_DIM, 1), lambda bi, h, qi, *_: (0, 0)),
        ],
        out_specs=pl.BlockSpec((1, rows, tq), lambda bi, h, qi, *_: (bi, h, qi)),
        scratch_shapes=[
            pltpu.VMEM((2, 2 * HEAD_DIM, tq), BF16),
            pltpu.VMEM((2, 1, tq), F32),
            pltpu.VMEM((2, vr, tq), F32),
        ],
    )
    return pl.pallas_call(
        kern, grid_spec=grid_spec,
        out_shape=jax.ShapeDtypeStruct(q_t.shape, BF16),
        compiler_params=_cparams(("parallel", "parallel", "arbitrary")),
        name="diff_attention",
    )(table, q_t, k, v_t, bias_tiles, lq1, lk1, lq2, lk2, subln_gain)


def _out_proj_kernel(a_ref, wo_ref, x_ref, mod_ref, o_ref):
    y = lax.dot_general(a_ref[0], wo_ref[...], (((0,), (0,)), ((), ())), preferred_element_type=F32)
    o_ref[0] = x_ref[0] + mod_ref[0, 0][2:3] * y


def _out_proj(attn_t, wo, x, mod, layer, seq_off, *, tm):
    b, l, d = x.shape
    return pl.pallas_call(
        _out_proj_kernel,
        grid=(b, l // tm),
        in_specs=[
            pl.BlockSpec((1, d, tm), lambda bi, ti: (bi, 0, ti)),
            pl.BlockSpec((d, d), lambda bi, ti: (0, 0)),
            pl.BlockSpec((1, tm, d), lambda bi, ti: (bi, ti, 0)),
            pl.BlockSpec((1, 1, N_MOD, d), lambda bi, ti: (layer, bi + seq_off, 0, 0)),
        ],
        out_specs=pl.BlockSpec((1, tm, d), lambda bi, ti: (bi, ti, 0)),
        out_shape=jax.ShapeDtypeStruct(x.shape, F32),
        compiler_params=_cparams(("parallel", "parallel")),
        name="out_proj",
    )(attn_t, wo, x, mod)


def _ffn_kernel(x_ref, mod_ref, g2_ref, wg_ref, wu_ref, wd_ref, o_ref, h_scr, acc_scr):
    j = pl.program_id(2)
    mod = mod_ref[0, 0]

    @pl.when(j == 0)
    def _():
        h_scr[...] = _rms_modulate(x_ref[0], g2_ref[0], mod[4:5], mod[3:4]).astype(BF16)
        acc_scr[...] = jnp.zeros(acc_scr.shape, F32)

    h = h_scr[...]
    g = jnp.dot(h, wg_ref[0], preferred_element_type=F32)
    u = jnp.dot(h, wu_ref[0], preferred_element_type=F32)
    a = ((g / (1.0 + jnp.exp(-g))) * u).astype(BF16)
    acc_scr[...] += jnp.dot(a, wd_ref[0], preferred_element_type=F32)

    @pl.when(j == pl.num_programs(2) - 1)
    def _():
        o_ref[0] = x_ref[0] + mod[5:6] * acc_scr[...]


def _ffn(x, mod, layer, seq_off, g2, w_gu, w_d, *, tm, n_chunks):
    b, l, d = x.shape
    fc = D_FF // n_chunks
    return pl.pallas_call(
        _ffn_kernel,
        grid=(b, l // tm, n_chunks),
        in_specs=[
            pl.BlockSpec((1, tm, d), lambda bi, ti, j: (bi, ti, 0)),
            pl.BlockSpec((1, 1, N_MOD, d), lambda bi, ti, j: (layer, bi + seq_off, 0, 0)),
            pl.BlockSpec((1, 1, d), lambda bi, ti, j: (layer, 0, 0)),
            pl.BlockSpec((1, d, fc), lambda bi, ti, j: (layer, 0, j)),
            pl.BlockSpec((1, d, fc), lambda bi, ti, j: (layer, 0, j + n_chunks)),
            pl.BlockSpec((1, fc, d), lambda bi, ti, j: (layer, j, 0)),
        ],
        out_specs=pl.BlockSpec((1, tm, d), lambda bi, ti, j: (bi, ti, 0)),
        out_shape=jax.ShapeDtypeStruct(x.shape, F32),
        scratch_shapes=[pltpu.VMEM((tm, d), BF16), pltpu.VMEM((tm, d), F32)],
        compiler_params=_cparams(("parallel", "parallel", "arbitrary")),
        name="ffn",
    )(x, mod, g2, w_gu, w_gu, w_d)


def _rope_tables(length):
    n_rows = length // GRID_W
    rr, cc = jnp.meshgrid(jnp.arange(n_rows), jnp.arange(GRID_W), indexing="ij")
    rows = rr.reshape(-1).astype(F32)
    cols = cc.reshape(-1).astype(F32)
    n_pairs = HEAD_DIM // 4
    inv_freq = ROPE_THETA ** (-jnp.arange(n_pairs, dtype=F32) / n_pairs)
    ang = jnp.concatenate([rows[:, None] * inv_freq[None], cols[:, None] * inv_freq[None]], axis=-1)
    return jnp.cos(ang).T, jnp.sin(ang).T


def _t5_bucket(rel):
    nb = NUM_BUCKETS // 2
    max_exact = nb // 2
    ret = (rel > 0).astype(jnp.int32) * nb
    n = jnp.abs(rel)
    large = max_exact + (jnp.log(jnp.maximum(n, 1).astype(F32) / max_exact)
                         / math.log(MAX_DISTANCE / max_exact) * (nb - max_exact)).astype(jnp.int32)
    large = jnp.minimum(large, nb - 1)
    return ret + jnp.where(n < max_exact, n, large)


def _near_bias_tiles(table, tq):
    assert KV_CHUNK >= MAX_DISTANCE - 1
    n_near = tq // KV_CHUNK + 2
    d = jnp.arange(n_near, dtype=jnp.int32)[:, None, None]
    kk = jnp.arange(KV_CHUNK, dtype=jnp.int32)[None, :, None]
    qq = jnp.arange(tq, dtype=jnp.int32)[None, None, :]
    rel = (d - 1) * KV_CHUNK + kk - qq
    return jnp.transpose(table[_t5_bucket(rel)], (3, 0, 1, 2))


def _deinterleave(n_heads):
    base = jnp.concatenate([jnp.arange(0, HEAD_DIM, 2), jnp.arange(1, HEAD_DIM, 2)])
    return (jnp.arange(n_heads)[:, None] * HEAD_DIM + base[None, :]).reshape(-1)


def _col(v):
    return v.astype(F32).reshape(-1, 1)


def _trunk(x, mod, seq_off, p, *, tm, tq):
    _, l, _ = x.shape
    cos_t, sin_t = _rope_tables(l)
    for layer in range(DEPTH):
        if layer % 2 == 0:
            q_t, k, v_t = _pre_attention(
                x, mod, layer, seq_off, p["g1"], p["a_wt"], p["a_gq"], p["a_gk"], cos_t, sin_t,
                n_q=A_HEADS, n_k=A_KV_HEADS, n_v=A_KV_HEADS, v_dim=HEAD_DIM, tm=tm)
            attn_t = _gqa_attention(q_t, k, v_t, tq=tq)
            wo = p["a_wo"]
        else:
            lambda_init = 0.8 - 0.6 * math.exp(-0.3 * layer)
            q_t, k, v_t = _pre_attention(
                x, mod, layer, seq_off, p["g1"], p["b_wt"], p["b_gq"], p["b_gk"], None, None,
                n_q=2 * B_HEADS, n_k=2 * B_HEADS, n_v=B_HEADS, v_dim=B_V_DIM, tm=tm)
            attn_t = _diff_attention(q_t, k, v_t, p["bias_tiles"], p["table"], p["lq1"], p["lk1"], p["lq2"],
                                     p["lk2"], p["subln"], tq=tq, lambda_init=lambda_init)
            wo = p["b_wo"]
        x = _out_proj(attn_t, wo, x, mod, layer, seq_off, tm=tm)
        x = _ffn(x, mod, layer, seq_off, p["g2"], p["w_gu"], p["w_d"], tm=tm, n_chunks=2)
    return x


def kernel(x_prompt, x_sample, c_prompt, c_sample, norm1_gain, norm2_gain, w_ada, b_ada, w_gate_up, w_down, rel_bias, a_w_qkv, a_w_o, a_q_gain, a_k_gain, b_w_qkv, b_w_o, b_q_gain, b_k_gain, b_lq1, b_lk1, b_lq2, b_lk2, b_subln_gain):
    tm, tq = 512, 512
    n_prompt = c_prompt.shape[0]
    n_seq = n_prompt + c_sample.shape[0]
    pad = (-n_seq) % 8
    c_all = jnp.concatenate([c_prompt, c_sample, jnp.zeros((pad, D_MODEL), F32)], axis=0)
    mod = _modulation(c_all, w_ada, b_ada).reshape(DEPTH, n_seq + pad, N_MOD, D_MODEL)

    perm_q = _deinterleave(A_HEADS)
    perm_k = A_HEADS * HEAD_DIM + _deinterleave(A_KV_HEADS)
    v_cols = jnp.arange((A_HEADS + A_KV_HEADS) * HEAD_DIM, (A_HEADS + 2 * A_KV_HEADS) * HEAD_DIM)
    a_perm = jnp.concatenate([perm_q, perm_k, v_cols])
    head_perm = _deinterleave(1)
    table = rel_bias.astype(F32)
    p = {
        "g1": norm1_gain.reshape(DEPTH, 1, D_MODEL),
        "g2": norm2_gain.reshape(DEPTH, 1, D_MODEL),
        "a_wt": a_w_qkv[0][:, a_perm].T.astype(BF16),
        "a_gq": _col(a_q_gain[0][head_perm]),
        "a_gk": _col(a_k_gain[0][head_perm]),
        "a_wo": a_w_o[0].astype(BF16),
        "b_wt": b_w_qkv[0].T.astype(BF16),
        "b_gq": _col(b_q_gain[0]),
        "b_gk": _col(b_k_gain[0]),
        "b_wo": b_w_o[0].astype(BF16),
        "lq1": b_lq1.astype(F32), "lk1": b_lk1.astype(F32), "lq2": b_lq2.astype(F32), "lk2": b_lk2.astype(F32),
        "subln": _col(b_subln_gain[0]),
        "table": table,
        "bias_tiles": _near_bias_tiles(table, tq),
        "w_gu": w_gate_up.astype(BF16),
        "w_d": w_down.astype(BF16),
    }
    y_prompt = _trunk(x_prompt, mod, 0, p, tm=tm, tq=tq)
    y_sample = _trunk(x_sample, mod, n_prompt, p, tm=tm, tq=tq)
    return (y_prompt, y_sample)
```

```python
import functools
import math

import jax
import jax.numpy as jnp
from jax import lax
from jax.experimental import pallas as pl
from jax.experimental.pallas import tpu as pltpu

F32 = jnp.float32
BF16 = jnp.bfloat16

D_MODEL = 1024
DEPTH = 2
GRID_W = 64
HEAD_DIM = 64
A_HEADS = 16
A_KV_HEADS = 4
B_HEADS = 8
B_V_DIM = 128
NUM_BUCKETS = 32
MAX_DISTANCE = 128
ROPE_THETA = 10000.0
D_FF = 2816
N_MOD = 6
EPS = 1e-6

LANES = 128
BF16_ROWS = 16
KV_CHUNK = 256
ONES_ROWS = BF16_ROWS
NEG_BIG = -1e30
VMEM_LIMIT = 56 * 1024 * 1024


def _cparams(sem):
    return pltpu.CompilerParams(dimension_semantics=sem, vmem_limit_bytes=VMEM_LIMIT)


def _mod_kernel(c_ref, w_ref, b_ref, o_ref):
    c = c_ref[...]
    c_act = (c / (1.0 + jnp.exp(-c))).astype(BF16)
    w = w_ref[0].astype(BF16)
    o_ref[0] = jnp.dot(c_act, w, preferred_element_type=F32) + b_ref[0]


def _modulation(c_all, w_ada, b_ada):
    s = c_all.shape[0]
    tn = 1024
    n_out = N_MOD * D_MODEL
    return pl.pallas_call(
        _mod_kernel,
        grid=(DEPTH, n_out // tn),
        in_specs=[
            pl.BlockSpec((s, D_MODEL), lambda i, n: (0, 0)),
            pl.BlockSpec((1, D_MODEL, tn), lambda i, n: (i, 0, n)),
            pl.BlockSpec((1, 1, tn), lambda i, n: (i, 0, n)),
        ],
        out_specs=pl.BlockSpec((1, s, tn), lambda i, n: (i, 0, n)),
        out_shape=jax.ShapeDtypeStruct((DEPTH, s, n_out), F32),
        compiler_params=_cparams(("arbitrary", "arbitrary")),
        name="adaln_mod",
    )(c_all, w_ada, b_ada.reshape(DEPTH, 1, n_out))


def _rms_modulate(x, gain, scale, shift):
    ms = jnp.mean(x * x, axis=-1, keepdims=True)
    y = x * lax.rsqrt(ms + EPS)
    return (y * gain) * (1.0 + scale) + shift


def _pre_kernel(x_ref, mod_ref, g1_ref, wt_ref, gq_ref, gk_ref, *rest, n_q, n_k, n_v, v_dim, rope):
    if rope:
        cos_ref, sin_ref, q_ref, k_ref, v_ref = rest
    else:
        q_ref, k_ref, v_ref = rest
    tm = x_ref.shape[1]
    mod = mod_ref[0, 0]
    h = _rms_modulate(x_ref[0], g1_ref[0], mod[1:2], mod[0:1]).astype(BF16)
    qkv_t = lax.dot_general(wt_ref[...], h, (((1,), (1,)), ((), ())), preferred_element_type=F32)

    def head_norm(blk, gain):
        ms = jnp.mean(blk * blk, axis=0, keepdims=True)
        y = blk * lax.rsqrt(ms + EPS) * gain
        if rope:
            half = HEAD_DIM // 2
            x1, x2 = y[:half], y[half:]
            c, s = cos_ref[...], sin_ref[...]
            y = jnp.concatenate([x1 * c - x2 * s, x1 * s + x2 * c], axis=0)
        return y

    gq, gk = gq_ref[...], gk_ref[...]
    for i in range(n_q):
        blk = head_norm(qkv_t[i * HEAD_DIM:(i + 1) * HEAD_DIM], gq)
        q_ref[0, i * HEAD_DIM:(i + 1) * HEAD_DIM, :] = (blk * (HEAD_DIM ** -0.5)).astype(BF16)
    k_rows = n_q * HEAD_DIM
    k_t = jnp.concatenate(
        [head_norm(qkv_t[k_rows + i * HEAD_DIM:k_rows + (i + 1) * HEAD_DIM], gk) for i in range(n_k)], axis=0)
    k_ref[0] = k_t.T.astype(BF16)
    v_rows = k_rows + n_k * HEAD_DIM
    ones_blk = (lax.broadcasted_iota(jnp.int32, (ONES_ROWS, KV_CHUNK), 0) == 0).astype(BF16)
    for g in range(n_v):
        v_t = qkv_t[v_rows + g * v_dim:v_rows + (g + 1) * v_dim].astype(BF16)
        for c in range(tm // KV_CHUNK):
            v_ref[0, g, c, 0:v_dim, :] = v_t[:, c * KV_CHUNK:(c + 1) * KV_CHUNK]
            v_ref[0, g, c, v_dim:v_dim + ONES_ROWS, :] = ones_blk


def _pre_attention(x, mod, layer, seq_off, g1, wt, gq, gk, cos_t, sin_t, *, n_q, n_k, n_v, v_dim, tm):
    b, l, d = x.shape
    rope = cos_t is not None
    n_rows = wt.shape[0]
    nc = l // KV_CHUNK
    in_specs = [
        pl.BlockSpec((1, tm, d), lambda bi, ti: (bi, ti, 0)),
        pl.BlockSpec((1, 1, N_MOD, d), lambda bi, ti: (layer, bi + seq_off, 0, 0)),
        pl.BlockSpec((1, 1, d), lambda bi, ti: (layer, 0, 0)),
        pl.BlockSpec((n_rows, d), lambda bi, ti: (0, 0)),
        pl.BlockSpec((HEAD_DIM, 1), lambda bi, ti: (0, 0)),
        pl.BlockSpec((HEAD_DIM, 1), lambda bi, ti: (0, 0)),
    ]
    args = [x, mod, g1, wt, gq, gk]
    if rope:
        in_specs += [pl.BlockSpec((HEAD_DIM // 2, tm), lambda bi, ti: (0, ti))] * 2
        args += [cos_t, sin_t]
    vr = v_dim + ONES_ROWS
    out_shape = (
        jax.ShapeDtypeStruct((b, n_q * HEAD_DIM, l), BF16),
        jax.ShapeDtypeStruct((b, l, n_k * HEAD_DIM), BF16),
        jax.ShapeDtypeStruct((b, n_v, nc, vr, KV_CHUNK), BF16),
    )
    out_specs = (
        pl.BlockSpec((1, n_q * HEAD_DIM, tm), lambda bi, ti: (bi, 0, ti)),
        pl.BlockSpec((1, tm, n_k * HEAD_DIM), lambda bi, ti: (bi, ti, 0)),
        pl.BlockSpec((1, n_v, tm // KV_CHUNK, vr, KV_CHUNK), lambda bi, ti: (bi, 0, ti, 0, 0)),
    )
    kern = functools.partial(_pre_kernel, n_q=n_q, n_k=n_k, n_v=n_v, v_dim=v_dim, rope=rope)
    return pl.pallas_call(
        kern, grid=(b, l // tm), in_specs=in_specs, out_specs=out_specs, out_shape=out_shape,
        compiler_params=_cparams(("parallel", "parallel")),
        name="pre_attn_rope" if rope else "pre_attn_diff",
    )(*args)


def _pad_queries(q_ref, qpad_scr, n_vh, half_of):
    tq = q_ref.shape[2]
    zeros = jnp.zeros((HEAD_DIM, tq), BF16)
    for i in range(n_vh):
        q = q_ref[0, i * HEAD_DIM:(i + 1) * HEAD_DIM, :]
        qpad_scr[i] = jnp.concatenate([q, zeros] if half_of(i) == 0 else [zeros, q], axis=0)


def _flash_step(s, v_t, m_old, acc_old, bias=None, const=None):
    if bias is not None:
        s = s + bias
    m_cur = jnp.max(s, axis=0, keepdims=True)
    if const is not None:
        m_new = jnp.maximum(m_old, m_cur + const)
        p = jnp.exp(s - (m_new - const))
    else:
        m_new = jnp.maximum(m_old, m_cur)
        p = jnp.exp(s - m_new)
    alpha = jnp.exp(m_old - m_new)
    pv = jnp.dot(v_t, p.astype(BF16), preferred_element_type=F32)
    return m_new, alpha * acc_old + pv


def _gqa_attn_kernel(q_ref, k_ref, v_ref, o_ref, qpad_scr, m_scr, acc_scr, *, n_vh, group):
    nc = v_ref.shape[2]
    _pad_queries(q_ref, qpad_scr, n_vh, lambda i: i // group)
    m_scr[...] = jnp.full(m_scr.shape, NEG_BIG, F32)
    acc_scr[...] = jnp.zeros(acc_scr.shape, F32)

    def body(j, carry):
        kc = k_ref[0, pl.ds(pl.multiple_of(j * KV_CHUNK, KV_CHUNK), KV_CHUNK), :]
        scores = [jnp.dot(kc, qpad_scr[i], preferred_element_type=F32) for i in range(n_vh)]
        for i in range(n_vh):
            m_new, acc_new = _flash_step(scores[i], v_ref[0, i // group, j], m_scr[i], acc_scr[i])
            m_scr[i] = m_new
            acc_scr[i] = acc_new
        return carry

    lax.fori_loop(0, nc, body, 0)
    for i in range(n_vh):
        acc = acc_scr[i]
        o_ref[0, i * HEAD_DIM:(i + 1) * HEAD_DIM, :] = (acc[:HEAD_DIM] / acc[HEAD_DIM:HEAD_DIM + 1]).astype(BF16)


def _gqa_attention(q_t, k, v_t, *, tq):
    b, _, l = q_t.shape
    group = A_HEADS // A_KV_HEADS
    n_vh = 2 * group
    rows = n_vh * HEAD_DIM
    nc = l // KV_CHUNK
    vr = HEAD_DIM + ONES_ROWS
    kern = functools.partial(_gqa_attn_kernel, n_vh=n_vh, group=group)
    return pl.pallas_call(
        kern,
        grid=(b, A_KV_HEADS // 2, l // tq),
        in_specs=[
            pl.BlockSpec((1, rows, tq), lambda bi, p, qi: (bi, p, qi)),
            pl.BlockSpec((1, l, LANES), lambda bi, p, qi: (bi, 0, p)),
            pl.BlockSpec((1, 2, nc, vr, KV_CHUNK), lambda bi, p, qi: (bi, p, 0, 0, 0)),
        ],
        out_specs=pl.BlockSpec((1, rows, tq), lambda bi, p, qi: (bi, p, qi)),
        out_shape=jax.ShapeDtypeStruct(q_t.shape, BF16),
        scratch_shapes=[
            pltpu.VMEM((n_vh, 2 * HEAD_DIM, tq), BF16),
            pltpu.VMEM((n_vh, 1, tq), F32),
            pltpu.VMEM((n_vh, vr, tq), F32),
        ],
        compiler_params=_cparams(("parallel", "parallel", "arbitrary")),
        name="gqa_attention",
    )(q_t, k, v_t)


def _diff_attn_kernel(tbl_ref, q_ref, k_ref, v_ref, bias_ref, lq1_ref, lk1_ref, lq2_ref, lk2_ref, sg_ref,
                      o_ref, qpad_scr, m_scr, acc_scr, *, lambda_init, n_near):
    nc = v_ref.shape[2]
    tq = q_ref.shape[2]
    h = pl.program_id(1)
    qi = pl.program_id(2)
    _pad_queries(q_ref, qpad_scr, 2, lambda i: i)
    m_scr[...] = jnp.full(m_scr.shape, NEG_BIG, F32)
    acc_scr[...] = jnp.zeros(acc_scr.shape, F32)

    def step(j, bias=None, const=None):
        kc = k_ref[0, pl.ds(pl.multiple_of(j * KV_CHUNK, KV_CHUNK), KV_CHUNK), :]
        v_t = v_ref[0, 0, j]
        scores = [jnp.dot(kc, qpad_scr[c], preferred_element_type=F32) for c in range(2)]
        for c in range(2):
            m_new, acc_new = _flash_step(scores[c], v_t, m_scr[c], acc_scr[c], bias=bias, const=const)
            m_scr[c] = m_new
            acc_scr[c] = acc_new

    q_per_k = tq // KV_CHUNK
    near_lo = qi * q_per_k - 1
    left_const = tbl_ref[NUM_BUCKETS // 2 - 1, h]
    right_const = tbl_ref[NUM_BUCKETS - 1, h]

    def far_left(j, carry):
        step(j, const=left_const)
        return carry

    def far_right(j, carry):
        step(j, const=right_const)
        return carry

    lax.fori_loop(0, jnp.maximum(near_lo, 0), far_left, 0)
    for d in range(n_near):
        j = near_lo + d

        @pl.when(jnp.logical_and(j >= 0, j < nc))
        def _():
            step(j, bias=bias_ref[0, d])

    lax.fori_loop(jnp.minimum(near_lo + n_near, nc), nc, far_right, 0)

    lam = (jnp.exp(jnp.sum(lq1_ref[...] * lk1_ref[...], axis=1, keepdims=True))
           - jnp.exp(jnp.sum(lq2_ref[...] * lk2_ref[...], axis=1, keepdims=True)) + lambda_init)
    a0, a1 = acc_scr[0], acc_scr[1]
    o = a0[:B_V_DIM] / a0[B_V_DIM:B_V_DIM + 1] - lam * (a1[:B_V_DIM] / a1[B_V_DIM:B_V_DIM + 1])
    ms = jnp.mean(o * o, axis=0, keepdims=True)
    y = (o * lax.rsqrt(ms + EPS)) * sg_ref[...]
    o_ref[0] = (y * (1.0 - lambda_init)).astype(BF16)


def _diff_attention(q_t, k, v_t, bias_tiles, table, lq1, lk1, lq2, lk2, subln_gain, *, tq, lambda_init):
    b, _, l = q_t.shape
    nc = l // KV_CHUNK
    vr = B_V_DIM + ONES_ROWS
    n_near = bias_tiles.shape[1]
    rows = 2 * HEAD_DIM
    kern = functools.partial(_diff_attn_kernel, lambda_init=lambda_init, n_near=n_near)
    vec = pl.BlockSpec((1, HEAD_DIM), lambda bi, h, qi, *_: (0, 0))
    grid_spec = pltpu.PrefetchScalarGridSpec(
        num_scalar_prefetch=1,
        grid=(b, B_HEADS, l // tq),
        in_specs=[
            pl.BlockSpec((1, rows, tq), lambda bi, h, qi, *_: (bi, h, qi)),
            pl.BlockSpec((1, l, LANES), lambda bi, h, qi, *_: (bi, 0, h)),
            pl.BlockSpec((1, 1, nc, vr, KV_CHUNK), lambda bi, h, qi, *_: (bi, h, 0, 0, 0)),
            pl.BlockSpec((1, n_near, KV_CHUNK, tq), lambda bi, h, qi, *_: (h, 0, 0, 0)),
            vec, vec, vec, vec,
            pl.BlockSpec((B_V_DIM, 1), lambda bi, h, qi, *_: (0, 0)),
        ],
        out_specs=pl.BlockSpec((1, rows, tq), lambda bi, h, qi, *_: (bi, h, qi)),
        scratch_shapes=[
            pltpu.VMEM((2, 2 * HEAD_DIM, tq), BF16),
            pltpu.VMEM((2, 1, tq), F32),
            pltpu.VMEM((2, vr, tq), F32),
        ],
    )
    return pl.pallas_call(
        kern, grid_spec=grid_spec,
        out_shape=jax.ShapeDtypeStruct(q_t.shape, BF16),
        compiler_params=_cparams(("parallel", "parallel", "arbitrary")),
        name="diff_attention",
    )(table, q_t, k, v_t, bias_tiles, lq1, lk1, lq2, lk2, subln_gain)


def _out_proj_kernel(a_ref, wo_ref, x_ref, mod_ref, o_ref):
    y = lax.dot_general(a_ref[0], wo_ref[...], (((0,), (0,)), ((), ())), preferred_element_type=F32)
    o_ref[0] = x_ref[0] + mod_ref[0, 0][2:3] * y


def _out_proj(attn_t, wo, x, mod, layer, seq_off, *, tm):
    b, l, d = x.shape
    return pl.pallas_call(
        _out_proj_kernel,
        grid=(b, l // tm),
        in_specs=[
            pl.BlockSpec((1, d, tm), lambda bi, ti: (bi, 0, ti)),
            pl.BlockSpec((d, d), lambda bi, ti: (0, 0)),
            pl.BlockSpec((1, tm, d), lambda bi, ti: (bi, ti, 0)),
            pl.BlockSpec((1, 1, N_MOD, d), lambda bi, ti: (layer, bi + seq_off, 0, 0)),
        ],
        out_specs=pl.BlockSpec((1, tm, d), lambda bi, ti: (bi, ti, 0)),
        out_shape=jax.ShapeDtypeStruct(x.shape, F32),
        compiler_params=_cparams(("parallel", "parallel")),
        name="out_proj",
    )(attn_t, wo, x, mod)


def _ffn_kernel(x_ref, mod_ref, g2_ref, wg_ref, wu_ref, wd_ref, o_ref, h_scr, acc_scr):
    j = pl.program_id(2)
    mod = mod_ref[0, 0]

    @pl.when(j == 0)
    def _():
        h_scr[...] = _rms_modulate(x_ref[0], g2_ref[0], mod[4:5], mod[3:4]).astype(BF16)
        acc_scr[...] = jnp.zeros(acc_scr.shape, F32)

    h = h_scr[...]
    g = jnp.dot(h, wg_ref[0], preferred_element_type=F32)
    u = jnp.dot(h, wu_ref[0], preferred_element_type=F32)
    a = ((g / (1.0 + jnp.exp(-g))) * u).astype(BF16)
    acc_scr[...] += jnp.dot(a, wd_ref[0], preferred_element_type=F32)

    @pl.when(j == pl.num_programs(2) - 1)
    def _():
        o_ref[0] = x_ref[0] + mod[5:6] * acc_scr[...]


def _ffn(x, mod, layer, seq_off, g2, w_gu, w_d, *, tm, n_chunks):
    b, l, d = x.shape
    fc = D_FF // n_chunks
    return pl.pallas_call(
        _ffn_kernel,
        grid=(b, l // tm, n_chunks),
        in_specs=[
            pl.BlockSpec((1, tm, d), lambda bi, ti, j: (bi, ti, 0)),
            pl.BlockSpec((1, 1, N_MOD, d), lambda bi, ti, j: (layer, bi + seq_off, 0, 0)),
            pl.BlockSpec((1, 1, d), lambda bi, ti, j: (layer, 0, 0)),
            pl.BlockSpec((1, d, fc), lambda bi, ti, j: (layer, 0, j)),
            pl.BlockSpec((1, d, fc), lambda bi, ti, j: (layer, 0, j + n_chunks)),
            pl.BlockSpec((1, fc, d), lambda bi, ti, j: (layer, j, 0)),
        ],
        out_specs=pl.BlockSpec((1, tm, d), lambda bi, ti, j: (bi, ti, 0)),
        out_shape=jax.ShapeDtypeStruct(x.shape, F32),
        scratch_shapes=[pltpu.VMEM((tm, d), BF16), pltpu.VMEM((tm, d), F32)],
        compiler_params=_cparams(("parallel", "parallel", "arbitrary")),
        name="ffn",
    )(x, mod, g2, w_gu, w_gu, w_d)


def _rope_tables(length):
    n_rows = length // GRID_W
    rr, cc = jnp.meshgrid(jnp.arange(n_rows), jnp.arange(GRID_W), indexing="ij")
    rows = rr.reshape(-1).astype(F32)
    cols = cc.reshape(-1).astype(F32)
    n_pairs = HEAD_DIM // 4
    inv_freq = ROPE_THETA ** (-jnp.arange(n_pairs, dtype=F32) / n_pairs)
    ang = jnp.concatenate([rows[:, None] * inv_freq[None], cols[:, None] * inv_freq[None]], axis=-1)
    return jnp.cos(ang).T, jnp.sin(ang).T


def _t5_bucket(rel):
    nb = NUM_BUCKETS // 2
    max_exact = nb // 2
    ret = (rel > 0).astype(jnp.int32) * nb
    n = jnp.abs(rel)
    large = max_exact + (jnp.log(jnp.maximum(n, 1).astype(F32) / max_exact)
                         / math.log(MAX_DISTANCE / max_exact) * (nb - max_exact)).astype(jnp.int32)
    large = jnp.minimum(large, nb - 1)
    return ret + jnp.where(n < max_exact, n, large)


def _near_bias_tiles(table, tq):
    assert KV_CHUNK >= MAX_DISTANCE - 1
    n_near = tq // KV_CHUNK + 2
    d = jnp.arange(n_near, dtype=jnp.int32)[:, None, None]
    kk = jnp.arange(KV_CHUNK, dtype=jnp.int32)[None, :, None]
    qq = jnp.arange(tq, dtype=jnp.int32)[None, None, :]
    rel = (d - 1) * KV_CHUNK + kk - qq
    return jnp.transpose(table[_t5_bucket(rel)], (3, 0, 1, 2))


def _deinterleave(n_heads):
    base = jnp.concatenate([jnp.arange(0, HEAD_DIM, 2), jnp.arange(1, HEAD_DIM, 2)])
    return (jnp.arange(n_heads)[:, None] * HEAD_DIM + base[None, :]).reshape(-1)


def _col(v):
    return v.astype(F32).reshape(-1, 1)


def _trunk(x, mod, seq_off, p, *, tm, tq):
    _, l, _ = x.shape
    cos_t, sin_t = _rope_tables(l)
    for layer in range(DEPTH):
        if layer % 2 == 0:
            q_t, k, v_t = _pre_attention(
                x, mod, layer, seq_off, p["g1"], p["a_wt"], p["a_gq"], p["a_gk"], cos_t, sin_t,
                n_q=A_HEADS, n_k=A_KV_HEADS, n_v=A_KV_HEADS, v_dim=HEAD_DIM, tm=tm)
            attn_t = _gqa_attention(q_t, k, v_t, tq=tq)
            wo = p["a_wo"]
        else:
            lambda_init = 0.8 - 0.6 * math.exp(-0.3 * layer)
            q_t, k, v_t = _pre_attention(
                x, mod, layer, seq_off, p["g1"], p["b_wt"], p["b_gq"], p["b_gk"], None, None,
                n_q=2 * B_HEADS, n_k=2 * B_HEADS, n_v=B_HEADS, v_dim=B_V_DIM, tm=tm)
            attn_t = _diff_attention(q_t, k, v_t, p["bias_tiles"], p["table"], p["lq1"], p["lk1"], p["lq2"],
                                     p["lk2"], p["subln"], tq=tq, lambda_init=lambda_init)
            wo = p["b_wo"]
        x = _out_proj(attn_t, wo, x, mod, layer, seq_off, tm=tm)
        x = _ffn(x, mod, layer, seq_off, p["g2"], p["w_gu"], p["w_d"], tm=tm, n_chunks=2)
    return x


def kernel(x_prompt, x_sample, c_prompt, c_sample, norm1_gain, norm2_gain, w_ada, b_ada, w_gate_up, w_down, rel_bias, a_w_qkv, a_w_o, a_q_gain, a_k_gain, b_w_qkv, b_w_o, b_q_gain, b_k_gain, b_lq1, b_lk1, b_lq2, b_lk2, b_subln_gain):
    tm, tq = 512, 512
    n_prompt = c_prompt.shape[0]
    n_seq = n_prompt + c_sample.shape[0]
    pad = (-n_seq) % 8
    c_all = jnp.concatenate([c_prompt, c_sample, jnp.zeros((pad, D_MODEL), F32)], axis=0)
    mod = _modulation(c_all, w_ada, b_ada).reshape(DEPTH, n_seq + pad, N_MOD, D_MODEL)

    perm_q = _deinterleave(A_HEADS)
    perm_k = A_HEADS * HEAD_DIM + _deinterleave(A_KV_HEADS)
    v_cols = jnp.arange((A_HEADS + A_KV_HEADS) * HEAD_DIM, (A_HEADS + 2 * A_KV_HEADS) * HEAD_DIM)
    a_perm = jnp.concatenate([perm_q, perm_k, v_cols])
    head_perm = _deinterleave(1)
    table = rel_bias.astype(F32)
    p = {
        "g1": norm1_gain.reshape(DEPTH, 1, D_MODEL),
        "g2": norm2_gain.reshape(DEPTH, 1, D_MODEL),
        "a_wt": a_w_qkv[0][:, a_perm].T.astype(BF16),
        "a_gq": _col(a_q_gain[0][head_perm]),
        "a_gk": _col(a_k_gain[0][head_perm]),
        "a_wo": a_w_o[0].astype(BF16),
        "b_wt": b_w_qkv[0].T.astype(BF16),
        "b_gq": _col(b_q_gain[0]),
        "b_gk": _col(b_k_gain[0]),
        "b_wo": b_w_o[0].astype(BF16),
        "lq1": b_lq1.astype(F32), "lk1": b_lk1.astype(F32), "lq2": b_lq2.astype(F32), "lk2": b_lk2.astype(F32),
        "subln": _col(b_subln_gain[0]),
        "table": table,
        "bias_tiles": _near_bias_tiles(table, tq),
        "w_gu": w_gate_up.astype(BF16),
        "w_d": w_down.astype(BF16),
    }
    y_prompt = _trunk(x_prompt, mod, 0, p, tm=tm, tq=tq)
    y_sample = _trunk(x_sample, mod, n_prompt, p, tm=tm, tq=tq)
    return (y_prompt, y_sample)
```

```python
import functools
import math

import jax
import jax.numpy as jnp
from jax import lax
from jax.experimental import pallas as pl
from jax.experimental.pallas import tpu as pltpu

F32 = jnp.float32
BF16 = jnp.bfloat16

D_MODEL = 1024
DEPTH = 2
GRID_W = 64
HEAD_DIM = 64
A_HEADS = 16
A_KV_HEADS = 4
B_HEADS = 8
B_V_DIM = 128
NUM_BUCKETS = 32
MAX_DISTANCE = 128
ROPE_THETA = 10000.0
D_FF = 2816
N_MOD = 6
EPS = 1e-6

LANES = 128
SUBLANES = 8
BF16_ROWS = 16
KV_CHUNK = 512
Q_BLOCK = KV_CHUNK
ONES_ROWS = BF16_ROWS
DIFF_HEADS_PER_STEP = 2
QK_LOOKAHEAD = 2
NEG_BIG = -1e30
SCORE_GUARD = 30.0
DENOM_FLOOR = 1e-30
VMEM_LIMIT = 56 * 1024 * 1024


def _cparams(sem):
    return pltpu.CompilerParams(dimension_semantics=sem, vmem_limit_bytes=VMEM_LIMIT)


def _mod_kernel(c_ref, w_ref, b_ref, o_ref):
    c = c_ref[...]
    c_act = (c / (1.0 + jnp.exp(-c))).astype(BF16)
    w = w_ref[0].astype(BF16)
    o_ref[0] = jnp.dot(c_act, w, preferred_element_type=F32) + b_ref[0]


def _modulation(c_all, w_ada, b_ada):
    s = c_all.shape[0]
    tn = 1024
    n_out = N_MOD * D_MODEL
    return pl.pallas_call(
        _mod_kernel,
        grid=(DEPTH, n_out // tn),
        in_specs=[
            pl.BlockSpec((s, D_MODEL), lambda i, n: (0, 0)),
            pl.BlockSpec((1, D_MODEL, tn), lambda i, n: (i, 0, n)),
            pl.BlockSpec((1, 1, tn), lambda i, n: (i, 0, n)),
        ],
        out_specs=pl.BlockSpec((1, s, tn), lambda i, n: (i, 0, n)),
        out_shape=jax.ShapeDtypeStruct((DEPTH, s, n_out), F32),
        compiler_params=_cparams(("arbitrary", "arbitrary")),
        name="adaln_mod",
    )(c_all, w_ada, b_ada.reshape(DEPTH, 1, n_out))


def _rms_modulate(x, gain, scale, shift):
    ms = jnp.mean(x * x, axis=-1, keepdims=True)
    y = x * lax.rsqrt(ms + EPS)
    return (y * gain) * (1.0 + scale) + shift


def _pre_kernel(x_ref, mod_ref, g1_ref, wt_ref, gq_ref, gk_ref, *rest, n_q, n_k, n_v, v_dim, rope):
    if rope:
        cos_ref, sin_ref, q_ref, k_ref, v_ref = rest
    else:
        q_ref, k_ref, v_ref = rest
    tm = x_ref.shape[1]
    mod = mod_ref[0, 0]
    h = _rms_modulate(x_ref[0], g1_ref[0], mod[1:2], mod[0:1]).astype(BF16)
    qkv_t = lax.dot_general(wt_ref[...], h, (((1,), (1,)), ((), ())), preferred_element_type=F32)

    def head_norm(blk, gain):
        ms = jnp.mean(blk * blk, axis=0, keepdims=True)
        y = blk * lax.rsqrt(ms + EPS) * gain
        if rope:
            half = HEAD_DIM // 2
            x1, x2 = y[:half], y[half:]
            c, s = cos_ref[...], sin_ref[...]
            y = jnp.concatenate([x1 * c - x2 * s, x1 * s + x2 * c], axis=0)
        return y

    gq, gk = gq_ref[...], gk_ref[...]
    for i in range(n_q):
        blk = head_norm(qkv_t[i * HEAD_DIM:(i + 1) * HEAD_DIM], gq)
        q_ref[0, i * HEAD_DIM:(i + 1) * HEAD_DIM, :] = (blk * (HEAD_DIM ** -0.5)).astype(BF16)
    k_rows = n_q * HEAD_DIM
    k_t = jnp.concatenate(
        [head_norm(qkv_t[k_rows + i * HEAD_DIM:k_rows + (i + 1) * HEAD_DIM], gk) for i in range(n_k)], axis=0)
    k_ref[0] = k_t.T.astype(BF16)
    v_rows = k_rows + n_k * HEAD_DIM
    ones_blk = (lax.broadcasted_iota(jnp.int32, (ONES_ROWS, KV_CHUNK), 0) == 0).astype(BF16)
    for g in range(n_v):
        v_t = qkv_t[v_rows + g * v_dim:v_rows + (g + 1) * v_dim].astype(BF16)
        for c in range(tm // KV_CHUNK):
            v_ref[0, g, c, 0:v_dim, :] = v_t[:, c * KV_CHUNK:(c + 1) * KV_CHUNK]
            v_ref[0, g, c, v_dim:v_dim + ONES_ROWS, :] = ones_blk


def _pre_attention(x, mod, layer, seq_off, g1, wt, gq, gk, cos_t, sin_t, *, n_q, n_k, n_v, v_dim, tm):
    b, l, d = x.shape
    rope = cos_t is not None
    n_rows = wt.shape[0]
    nc = l // KV_CHUNK
    in_specs = [
        pl.BlockSpec((1, tm, d), lambda bi, ti: (bi, ti, 0)),
        pl.BlockSpec((1, 1, N_MOD, d), lambda bi, ti: (layer, bi + seq_off, 0, 0)),
        pl.BlockSpec((1, 1, d), lambda bi, ti: (layer, 0, 0)),
        pl.BlockSpec((n_rows, d), lambda bi, ti: (0, 0)),
        pl.BlockSpec((HEAD_DIM, 1), lambda bi, ti: (0, 0)),
        pl.BlockSpec((HEAD_DIM, 1), lambda bi, ti: (0, 0)),
    ]
    args = [x, mod, g1, wt, gq, gk]
    if rope:
        in_specs += [pl.BlockSpec((HEAD_DIM // 2, tm), lambda bi, ti: (0, ti))] * 2
        args += [cos_t, sin_t]
    vr = v_dim + ONES_ROWS
    out_shape = (
        jax.ShapeDtypeStruct((b, n_q * HEAD_DIM, l), BF16),
        jax.ShapeDtypeStruct((b, l, n_k * HEAD_DIM), BF16),
        jax.ShapeDtypeStruct((b, n_v, nc, vr, KV_CHUNK), BF16),
    )
    out_specs = (
        pl.BlockSpec((1, n_q * HEAD_DIM, tm), lambda bi, ti: (bi, 0, ti)),
        pl.BlockSpec((1, tm, n_k * HEAD_DIM), lambda bi, ti: (bi, ti, 0)),
        pl.BlockSpec((1, n_v, tm // KV_CHUNK, vr, KV_CHUNK), lambda bi, ti: (bi, 0, ti, 0, 0)),
    )
    kern = functools.partial(_pre_kernel, n_q=n_q, n_k=n_k, n_v=n_v, v_dim=v_dim, rope=rope)
    return pl.pallas_call(
        kern, grid=(b, l // tm), in_specs=in_specs, out_specs=out_specs, out_shape=out_shape,
        compiler_params=_cparams(("parallel", "parallel")),
        name="pre_attn_rope" if rope else "pre_attn_diff",
    )(*args)


def _attend(q_ref, k_ref, v_ref, qpad_scr, m_scr, acc_scr, *, n_items, k_tile, k_half, v_index, bias_tile,
            write_out):
    nc = v_ref.shape[2]
    tq = q_ref.shape[2]
    assert nc % 2 == 0
    zeros = jnp.zeros((HEAD_DIM, tq), BF16)
    for i in range(n_items):
        q = q_ref[0, i * HEAD_DIM:(i + 1) * HEAD_DIM, :]
        qpad_scr[i] = jnp.concatenate([q, zeros] if k_half(i) == 0 else [zeros, q], axis=0)

    def scores(j, i):
        t = k_tile(i)
        kc = k_ref[0, pl.ds(pl.multiple_of(j * KV_CHUNK, KV_CHUNK), KV_CHUNK), t * LANES:(t + 1) * LANES]
        s = jnp.dot(kc, qpad_scr[i], preferred_element_type=F32)
        b = bias_tile(i, j)
        return s if b is None else s + b

    def reset():
        m_scr[...] = jnp.full(m_scr.shape, NEG_BIG, F32)
        acc_scr[...] = jnp.zeros(acc_scr.shape, F32)

    reset()

    def fast_pair(t, carry):
        seq = [(2 * t + u, i) for u in range(2) for i in range(n_items)]
        pending = [scores(*seq[n]) for n in range(QK_LOOKAHEAD)]
        for n, (j, i) in enumerate(seq):
            s = pending[n]
            if n + QK_LOOKAHEAD < len(seq):
                pending.append(scores(*seq[n + QK_LOOKAHEAD]))
            m_scr[i] = jnp.maximum(m_scr[i], jnp.max(s.reshape(-1, SUBLANES, tq), axis=0))
            p = jnp.exp(s).astype(BF16)
            acc_scr[i] += jnp.dot(v_ref[0, v_index(i), j], p, preferred_element_type=F32)
        return carry

    lax.fori_loop(0, nc // 2, fast_pair, 0)
    l_min = write_out(acc_scr)
    in_range = jnp.logical_and(jnp.max(m_scr[...]) <= SCORE_GUARD, l_min >= DENOM_FLOOR)

    @pl.when(jnp.logical_not(in_range))
    def _():
        reset()

        def safe_chunk(j, carry):
            tiles = [scores(j, i) for i in range(n_items)]
            for i in range(n_items):
                s = tiles[i]
                m_old = m_scr[i, 0:1]
                m_new = jnp.maximum(m_old, jnp.max(s, axis=0, keepdims=True))
                p = jnp.exp(s - m_new).astype(BF16)
                pv = jnp.dot(v_ref[0, v_index(i), j], p, preferred_element_type=F32)
                acc_scr[i] = jnp.exp(m_old - m_new) * acc_scr[i] + pv
                m_scr[i, 0:1] = m_new
            return carry

        lax.fori_loop(0, nc, safe_chunk, 0)
        write_out(acc_scr)


def _gqa_attn_kernel(q_ref, k_ref, v_ref, o_ref, qpad_scr, m_scr, acc_scr, *, n_items, group):
    def write_out(acc_ref):
        l_min = jnp.full((1, 1), jnp.inf, F32)
        for i in range(n_items):
            acc = acc_ref[i]
            l = acc[HEAD_DIM:HEAD_DIM + 1]
            l_min = jnp.minimum(l_min, jnp.min(l, axis=1, keepdims=True))
            o_ref[0, i * HEAD_DIM:(i + 1) * HEAD_DIM, :] = (acc[:HEAD_DIM] / l).astype(BF16)
        return l_min[0, 0]

    _attend(q_ref, k_ref, v_ref, qpad_scr, m_scr, acc_scr, n_items=n_items,
            k_tile=lambda i: 0, k_half=lambda i: i // group, v_index=lambda i: i // group,
            bias_tile=lambda i, j: None, write_out=write_out)


def _gqa_attention(q_t, k, v_t):
    b, _, l = q_t.shape
    tq = Q_BLOCK
    group = A_HEADS // A_KV_HEADS
    n_items = 2 * group
    rows = n_items * HEAD_DIM
    nc = l // KV_CHUNK
    vr = HEAD_DIM + ONES_ROWS
    kern = functools.partial(_gqa_attn_kernel, n_items=n_items, group=group)
    return pl.pallas_call(
        kern,
        grid=(b, A_KV_HEADS // 2, l // tq),
        in_specs=[
            pl.BlockSpec((1, rows, tq), lambda bi, p, qi: (bi, p, qi)),
            pl.BlockSpec((1, l, LANES), lambda bi, p, qi: (bi, 0, p)),
            pl.BlockSpec((1, 2, nc, vr, KV_CHUNK), lambda bi, p, qi: (bi, p, 0, 0, 0)),
        ],
        out_specs=pl.BlockSpec((1, rows, tq), lambda bi, p, qi: (bi, p, qi)),
        out_shape=jax.ShapeDtypeStruct(q_t.shape, BF16),
        scratch_shapes=[
            pltpu.VMEM((n_items, 2 * HEAD_DIM, tq), BF16),
            pltpu.VMEM((n_items, SUBLANES, tq), F32),
            pltpu.VMEM((n_items, vr, tq), F32),
        ],
        compiler_params=_cparams(("parallel", "parallel", "arbitrary")),
        name="gqa_attention",
    )(q_t, k, v_t)


def _diff_attn_kernel(q_ref, k_ref, v_ref, bias_ref, lq1_ref, lk1_ref, lq2_ref, lk2_ref, sg_ref,
                      o_ref, qpad_scr, m_scr, acc_scr, *, lambda_init, n_heads):
    qi = pl.program_id(2)
    n_tiles = bias_ref.shape[1]

    def bias_tile(i, j):
        return bias_ref[i // 2, jnp.clip(j - qi + n_tiles // 2, 0, n_tiles - 1)]

    lam = (jnp.exp(jnp.sum(lq1_ref[...] * lk1_ref[...], axis=1, keepdims=True))
           - jnp.exp(jnp.sum(lq2_ref[...] * lk2_ref[...], axis=1, keepdims=True)) + lambda_init)

    def write_out(acc_ref):
        l_min = jnp.full((1, 1), jnp.inf, F32)
        for hh in range(n_heads):
            a0, a1 = acc_ref[2 * hh], acc_ref[2 * hh + 1]
            l0, l1 = a0[B_V_DIM:B_V_DIM + 1], a1[B_V_DIM:B_V_DIM + 1]
            l_min = jnp.minimum(l_min, jnp.min(jnp.minimum(l0, l1), axis=1, keepdims=True))
            o = a0[:B_V_DIM] / l0 - lam * (a1[:B_V_DIM] / l1)
            ms = jnp.mean(o * o, axis=0, keepdims=True)
            y = (o * lax.rsqrt(ms + EPS)) * sg_ref[...]
            o_ref[0, hh * B_V_DIM:(hh + 1) * B_V_DIM, :] = (y * (1.0 - lambda_init)).astype(BF16)
        return l_min[0, 0]

    _attend(q_ref, k_ref, v_ref, qpad_scr, m_scr, acc_scr, n_items=2 * n_heads,
            k_tile=lambda i: i // 2, k_half=lambda i: i % 2, v_index=lambda i: i // 2,
            bias_tile=bias_tile, write_out=write_out)


def _diff_attention(q_t, k, v_t, bias_tiles, lq1, lk1, lq2, lk2, subln_gain, *, lambda_init):
    b, _, l = q_t.shape
    tq = Q_BLOCK
    nh = DIFF_HEADS_PER_STEP
    nc = l // KV_CHUNK
    vr = B_V_DIM + ONES_ROWS
    n_tiles = bias_tiles.shape[1]
    rows = nh * 2 * HEAD_DIM
    kern = functools.partial(_diff_attn_kernel, lambda_init=lambda_init, n_heads=nh)
    vec = pl.BlockSpec((1, HEAD_DIM), lambda bi, hp, qi: (0, 0))
    once = pl.Buffered(1)
    return pl.pallas_call(
        kern,
        grid=(b, B_HEADS // nh, l // tq),
        in_specs=[
            pl.BlockSpec((1, rows, tq), lambda bi, hp, qi: (bi, hp, qi)),
            pl.BlockSpec((1, l, nh * LANES), lambda bi, hp, qi: (bi, 0, hp), pipeline_mode=once),
            pl.BlockSpec((1, nh, nc, vr, KV_CHUNK), lambda bi, hp, qi: (bi, hp, 0, 0, 0), pipeline_mode=once),
            pl.BlockSpec((nh, n_tiles, KV_CHUNK, tq), lambda bi, hp, qi: (hp, 0, 0, 0), pipeline_mode=once),
            vec, vec, vec, vec,
            pl.BlockSpec((B_V_DIM, 1), lambda bi, hp, qi: (0, 0)),
        ],
        out_specs=pl.BlockSpec((1, rows, tq), lambda bi, hp, qi: (bi, hp, qi)),
        out_shape=jax.ShapeDtypeStruct(q_t.shape, BF16),
        scratch_shapes=[
            pltpu.VMEM((2 * nh, 2 * HEAD_DIM, tq), BF16),
            pltpu.VMEM((2 * nh, SUBLANES, tq), F32),
            pltpu.VMEM((2 * nh, vr, tq), F32),
        ],
        compiler_params=_cparams(("parallel", "parallel", "arbitrary")),
        name="diff_attention",
    )(q_t, k, v_t, bias_tiles, lq1, lk1, lq2, lk2, subln_gain)


def _bias_tile_kernel(tbl_ref, bkt_ref, o_ref):
    h = pl.program_id(0)
    bkt = bkt_ref[0]
    acc = jnp.zeros(bkt.shape, F32)
    for b in range(NUM_BUCKETS):
        acc = jnp.where(bkt == b, tbl_ref[b, h], acc)
    o_ref[0, 0] = acc


def _bias_tiles(table, buckets):
    n_t, nk, nq = buckets.shape
    n_h = table.shape[1]
    grid_spec = pltpu.PrefetchScalarGridSpec(
        num_scalar_prefetch=1,
        grid=(n_h, n_t),
        in_specs=[pl.BlockSpec((1, nk, nq), lambda h, t, *_: (t, 0, 0))],
        out_specs=pl.BlockSpec((1, 1, nk, nq), lambda h, t, *_: (h, t, 0, 0)),
    )
    return pl.pallas_call(
        _bias_tile_kernel, grid_spec=grid_spec,
        out_shape=jax.ShapeDtypeStruct((n_h, n_t, nk, nq), F32),
        compiler_params=_cparams(("arbitrary", "arbitrary")),
        name="t5_bias_tiles",
    )(table, buckets)


def _out_proj_kernel(a_ref, wo_ref, x_ref, mod_ref, o_ref):
    y = lax.dot_general(a_ref[0], wo_ref[...], (((0,), (0,)), ((), ())), preferred_element_type=F32)
    o_ref[0] = x_ref[0] + mod_ref[0, 0][2:3] * y


def _out_proj(attn_t, wo, x, mod, layer, seq_off, *, tm):
    b, l, d = x.shape
    return pl.pallas_call(
        _out_proj_kernel,
        grid=(b, l // tm),
        in_specs=[
            pl.BlockSpec((1, d, tm), lambda bi, ti: (bi, 0, ti)),
            pl.BlockSpec((d, d), lambda bi, ti: (0, 0)),
            pl.BlockSpec((1, tm, d), lambda bi, ti: (bi, ti, 0)),
            pl.BlockSpec((1, 1, N_MOD, d), lambda bi, ti: (layer, bi + seq_off, 0, 0)),
        ],
        out_specs=pl.BlockSpec((1, tm, d), lambda bi, ti: (bi, ti, 0)),
        out_shape=jax.ShapeDtypeStruct(x.shape, F32),
        compiler_params=_cparams(("parallel", "parallel")),
        name="out_proj",
    )(attn_t, wo, x, mod)


def _ffn_kernel(x_ref, mod_ref, g2_ref, wg_ref, wu_ref, wd_ref, o_ref, h_scr, acc_scr):
    j = pl.program_id(2)
    mod = mod_ref[0, 0]

    @pl.when(j == 0)
    def _():
        h_scr[...] = _rms_modulate(x_ref[0], g2_ref[0], mod[4:5], mod[3:4]).astype(BF16)
        acc_scr[...] = jnp.zeros(acc_scr.shape, F32)

    h = h_scr[...]
    g = jnp.dot(h, wg_ref[0], preferred_element_type=F32)
    u = jnp.dot(h, wu_ref[0], preferred_element_type=F32)
    a = ((g / (1.0 + jnp.exp(-g))) * u).astype(BF16)
    acc_scr[...] += jnp.dot(a, wd_ref[0], preferred_element_type=F32)

    @pl.when(j == pl.num_programs(2) - 1)
    def _():
        o_ref[0] = x_ref[0] + mod[5:6] * acc_scr[...]


def _ffn(x, mod, layer, seq_off, g2, w_gu, w_d, *, tm, n_chunks):
    b, l, d = x.shape
    fc = D_FF // n_chunks
    return pl.pallas_call(
        _ffn_kernel,
        grid=(b, l // tm, n_chunks),
        in_specs=[
            pl.BlockSpec((1, tm, d), lambda bi, ti, j: (bi, ti, 0)),
            pl.BlockSpec((1, 1, N_MOD, d), lambda bi, ti, j: (layer, bi + seq_off, 0, 0)),
            pl.BlockSpec((1, 1, d), lambda bi, ti, j: (layer, 0, 0)),
            pl.BlockSpec((1, d, fc), lambda bi, ti, j: (layer, 0, j)),
            pl.BlockSpec((1, d, fc), lambda bi, ti, j: (layer, 0, j + n_chunks)),
            pl.BlockSpec((1, fc, d), lambda bi, ti, j: (layer, j, 0)),
        ],
        out_specs=pl.BlockSpec((1, tm, d), lambda bi, ti, j: (bi, ti, 0)),
        out_shape=jax.ShapeDtypeStruct(x.shape, F32),
        scratch_shapes=[pltpu.VMEM((tm, d), BF16), pltpu.VMEM((tm, d), F32)],
        compiler_params=_cparams(("parallel", "parallel", "arbitrary")),
        name="ffn",
    )(x, mod, g2, w_gu, w_gu, w_d)


def _rope_tables(length):
    n_rows = length // GRID_W
    rr, cc = jnp.meshgrid(jnp.arange(n_rows), jnp.arange(GRID_W), indexing="ij")
    rows = rr.reshape(-1).astype(F32)
    cols = cc.reshape(-1).astype(F32)
    n_pairs = HEAD_DIM // 4
    inv_freq = ROPE_THETA ** (-jnp.arange(n_pairs, dtype=F32) / n_pairs)
    ang = jnp.concatenate([rows[:, None] * inv_freq[None], cols[:, None] * inv_freq[None]], axis=-1)
    return jnp.cos(ang).T, jnp.sin(ang).T


def _t5_bucket(rel):
    nb = NUM_BUCKETS // 2
    max_exact = nb // 2
    ret = (rel > 0).astype(jnp.int32) * nb
    n = jnp.abs(rel)
    large = max_exact + (jnp.log(jnp.maximum(n, 1).astype(F32) / max_exact)
                         / math.log(MAX_DISTANCE / max_exact) * (nb - max_exact)).astype(jnp.int32)
    large = jnp.minimum(large, nb - 1)
    return ret + jnp.where(n < max_exact, n, large)


def _bias_buckets():
    assert Q_BLOCK == KV_CHUNK and KV_CHUNK >= MAX_DISTANCE
    off = jnp.arange(-2, 3, dtype=jnp.int32)[:, None, None]
    kk = jnp.arange(KV_CHUNK, dtype=jnp.int32)[None, :, None]
    qq = jnp.arange(Q_BLOCK, dtype=jnp.int32)[None, None, :]
    return _t5_bucket(off * KV_CHUNK + kk - qq)


def _deinterleave(n_heads):
    base = jnp.concatenate([jnp.arange(0, HEAD_DIM, 2), jnp.arange(1, HEAD_DIM, 2)])
    return (jnp.arange(n_heads)[:, None] * HEAD_DIM + base[None, :]).reshape(-1)


def _col(v):
    return v.astype(F32).reshape(-1, 1)


def _trunk(x, mod, seq_off, p, *, tm):
    _, l, _ = x.shape
    cos_t, sin_t = _rope_tables(l)
    for layer in range(DEPTH):
        if layer % 2 == 0:
            q_t, k, v_t = _pre_attention(
                x, mod, layer, seq_off, p["g1"], p["a_wt"], p["a_gq"], p["a_gk"], cos_t, sin_t,
                n_q=A_HEADS, n_k=A_KV_HEADS, n_v=A_KV_HEADS, v_dim=HEAD_DIM, tm=tm)
            attn_t = _gqa_attention(q_t, k, v_t)
            wo = p["a_wo"]
        else:
            lambda_init = 0.8 - 0.6 * math.exp(-0.3 * layer)
            q_t, k, v_t = _pre_attention(
                x, mod, layer, seq_off, p["g1"], p["b_wt"], p["b_gq"], p["b_gk"], None, None,
                n_q=2 * B_HEADS, n_k=2 * B_HEADS, n_v=B_HEADS, v_dim=B_V_DIM, tm=tm)
            attn_t = _diff_attention(q_t, k, v_t, p["bias_tiles"], p["lq1"], p["lk1"], p["lq2"], p["lk2"],
                                     p["subln"], lambda_init=lambda_init)
            wo = p["b_wo"]
        x = _out_proj(attn_t, wo, x, mod, layer, seq_off, tm=tm)
        x = _ffn(x, mod, layer, seq_off, p["g2"], p["w_gu"], p["w_d"], tm=tm, n_chunks=2)
    return x


def kernel(x_prompt, x_sample, c_prompt, c_sample, norm1_gain, norm2_gain, w_ada, b_ada, w_gate_up, w_down, rel_bias, a_w_qkv, a_w_o, a_q_gain, a_k_gain, b_w_qkv, b_w_o, b_q_gain, b_k_gain, b_lq1, b_lk1, b_lq2, b_lk2, b_subln_gain):
    tm = 512
    n_prompt = c_prompt.shape[0]
    n_seq = n_prompt + c_sample.shape[0]
    pad = (-n_seq) % 8
    c_all = jnp.concatenate([c_prompt, c_sample, jnp.zeros((pad, D_MODEL), F32)], axis=0)
    mod = _modulation(c_all, w_ada, b_ada).reshape(DEPTH, n_seq + pad, N_MOD, D_MODEL)

    perm_q = _deinterleave(A_HEADS)
    perm_k = A_HEADS * HEAD_DIM + _deinterleave(A_KV_HEADS)
    v_cols = jnp.arange((A_HEADS + A_KV_HEADS) * HEAD_DIM, (A_HEADS + 2 * A_KV_HEADS) * HEAD_DIM)
    a_perm = jnp.concatenate([perm_q, perm_k, v_cols])
    head_perm = _deinterleave(1)
    p = {
        "g1": norm1_gain.reshape(DEPTH, 1, D_MODEL),
        "g2": norm2_gain.reshape(DEPTH, 1, D_MODEL),
        "a_wt": a_w_qkv[0][:, a_perm].T.astype(BF16),
        "a_gq": _col(a_q_gain[0][head_perm]),
        "a_gk": _col(a_k_gain[0][head_perm]),
        "a_wo": a_w_o[0].astype(BF16),
        "b_wt": b_w_qkv[0].T.astype(BF16),
        "b_gq": _col(b_q_gain[0]),
        "b_gk": _col(b_k_gain[0]),
        "b_wo": b_w_o[0].astype(BF16),
        "lq1": b_lq1.astype(F32), "lk1": b_lk1.astype(F32), "lq2": b_lq2.astype(F32), "lk2": b_lk2.astype(F32),
        "subln": _col(b_subln_gain[0]),
        "bias_tiles": _bias_tiles(rel_bias.astype(F32), _bias_buckets()),
        "w_gu": w_gate_up.astype(BF16),
        "w_d": w_down.astype(BF16),
    }
    y_prompt = _trunk(x_prompt, mod, 0, p, tm=tm)
    y_sample = _trunk(x_sample, mod, n_prompt, p, tm=tm)
    return (y_prompt, y_sample)
```

```python
import functools
import math

import jax
import jax.numpy as jnp
from jax import lax
from jax.experimental import pallas as pl
from jax.experimental.pallas import tpu as pltpu

F32 = jnp.float32
BF16 = jnp.bfloat16

D_MODEL = 1024
DEPTH = 2
GRID_W = 64
HEAD_DIM = 64
A_HEADS = 16
A_KV_HEADS = 4
B_HEADS = 8
B_V_DIM = 128
NUM_BUCKETS = 32
MAX_DISTANCE = 128
ROPE_THETA = 10000.0
D_FF = 2816
N_MOD = 6
EPS = 1e-6

LANES = 128
SUBLANES = 8
BF16_ROWS = 16
KV_CHUNK = 256
Q_BLOCK = 512
CHUNKS_PER_BODY = 4
ONES_ROWS = BF16_ROWS
DIFF_HEADS_PER_STEP = 2
QK_LOOKAHEAD = 2
NEG_BIG = -1e30
SCORE_GUARD = 30.0
DENOM_FLOOR = 1e-30
VMEM_LIMIT = 56 * 1024 * 1024


def _cparams(sem):
    return pltpu.CompilerParams(dimension_semantics=sem, vmem_limit_bytes=VMEM_LIMIT)


def _mod_kernel(c_ref, w_ref, b_ref, o_ref):
    c = c_ref[...]
    c_act = (c / (1.0 + jnp.exp(-c))).astype(BF16)
    w = w_ref[0].astype(BF16)
    o_ref[0] = jnp.dot(c_act, w, preferred_element_type=F32) + b_ref[0]


def _modulation(c_all, w_ada, b_ada):
    s = c_all.shape[0]
    tn = 1024
    n_out = N_MOD * D_MODEL
    return pl.pallas_call(
        _mod_kernel,
        grid=(DEPTH, n_out // tn),
        in_specs=[
            pl.BlockSpec((s, D_MODEL), lambda i, n: (0, 0)),
            pl.BlockSpec((1, D_MODEL, tn), lambda i, n: (i, 0, n)),
            pl.BlockSpec((1, 1, tn), lambda i, n: (i, 0, n)),
        ],
        out_specs=pl.BlockSpec((1, s, tn), lambda i, n: (i, 0, n)),
        out_shape=jax.ShapeDtypeStruct((DEPTH, s, n_out), F32),
        compiler_params=_cparams(("arbitrary", "arbitrary")),
        name="adaln_mod",
    )(c_all, w_ada, b_ada.reshape(DEPTH, 1, n_out))


def _rms_modulate(x, gain, scale, shift):
    ms = jnp.mean(x * x, axis=-1, keepdims=True)
    y = x * lax.rsqrt(ms + EPS)
    return (y * gain) * (1.0 + scale) + shift


def _pre_kernel(x_ref, mod_ref, g1_ref, wt_ref, gq_ref, gk_ref, *rest, n_q, n_k, n_v, v_dim, rope):
    if rope:
        cos_ref, sin_ref, q_ref, k_ref, v_ref = rest
    else:
        q_ref, k_ref, v_ref = rest
    tm = x_ref.shape[1]
    mod = mod_ref[0, 0]
    h = _rms_modulate(x_ref[0], g1_ref[0], mod[1:2], mod[0:1]).astype(BF16)
    qkv_t = lax.dot_general(wt_ref[...], h, (((1,), (1,)), ((), ())), preferred_element_type=F32)

    def head_norm(blk, gain):
        ms = jnp.mean(blk * blk, axis=0, keepdims=True)
        y = blk * lax.rsqrt(ms + EPS) * gain
        if rope:
            half = HEAD_DIM // 2
            x1, x2 = y[:half], y[half:]
            c, s = cos_ref[...], sin_ref[...]
            y = jnp.concatenate([x1 * c - x2 * s, x1 * s + x2 * c], axis=0)
        return y

    gq, gk = gq_ref[...], gk_ref[...]
    for i in range(n_q):
        blk = head_norm(qkv_t[i * HEAD_DIM:(i + 1) * HEAD_DIM], gq)
        q_ref[0, i * HEAD_DIM:(i + 1) * HEAD_DIM, :] = (blk * (HEAD_DIM ** -0.5)).astype(BF16)
    k_rows = n_q * HEAD_DIM
    k_t = jnp.concatenate(
        [head_norm(qkv_t[k_rows + i * HEAD_DIM:k_rows + (i + 1) * HEAD_DIM], gk) for i in range(n_k)], axis=0)
    k_ref[0] = k_t.T.astype(BF16)
    v_rows = k_rows + n_k * HEAD_DIM
    ones_blk = (lax.broadcasted_iota(jnp.int32, (ONES_ROWS, KV_CHUNK), 0) == 0).astype(BF16)
    for g in range(n_v):
        v_t = qkv_t[v_rows + g * v_dim:v_rows + (g + 1) * v_dim].astype(BF16)
        for c in range(tm // KV_CHUNK):
            v_ref[0, g, c, 0:v_dim, :] = v_t[:, c * KV_CHUNK:(c + 1) * KV_CHUNK]
            v_ref[0, g, c, v_dim:v_dim + ONES_ROWS, :] = ones_blk


def _pre_attention(x, mod, layer, seq_off, g1, wt, gq, gk, cos_t, sin_t, *, n_q, n_k, n_v, v_dim, tm):
    b, l, d = x.shape
    rope = cos_t is not None
    n_rows = wt.shape[0]
    nc = l // KV_CHUNK
    in_specs = [
        pl.BlockSpec((1, tm, d), lambda bi, ti: (bi, ti, 0)),
        pl.BlockSpec((1, 1, N_MOD, d), lambda bi, ti: (layer, bi + seq_off, 0, 0)),
        pl.BlockSpec((1, 1, d), lambda bi, ti: (layer, 0, 0)),
        pl.BlockSpec((n_rows, d), lambda bi, ti: (0, 0)),
        pl.BlockSpec((HEAD_DIM, 1), lambda bi, ti: (0, 0)),
        pl.BlockSpec((HEAD_DIM, 1), lambda bi, ti: (0, 0)),
    ]
    args = [x, mod, g1, wt, gq, gk]
    if rope:
        in_specs += [pl.BlockSpec((HEAD_DIM // 2, tm), lambda bi, ti: (0, ti))] * 2
        args += [cos_t, sin_t]
    vr = v_dim + ONES_ROWS
    out_shape = (
        jax.ShapeDtypeStruct((b, n_q * HEAD_DIM, l), BF16),
        jax.ShapeDtypeStruct((b, l, n_k * HEAD_DIM), BF16),
        jax.ShapeDtypeStruct((b, n_v, nc, vr, KV_CHUNK), BF16),
    )
    out_specs = (
        pl.BlockSpec((1, n_q * HEAD_DIM, tm), lambda bi, ti: (bi, 0, ti)),
        pl.BlockSpec((1, tm, n_k * HEAD_DIM), lambda bi, ti: (bi, ti, 0)),
        pl.BlockSpec((1, n_v, tm // KV_CHUNK, vr, KV_CHUNK), lambda bi, ti: (bi, 0, ti, 0, 0)),
    )
    kern = functools.partial(_pre_kernel, n_q=n_q, n_k=n_k, n_v=n_v, v_dim=v_dim, rope=rope)
    return pl.pallas_call(
        kern, grid=(b, l // tm), in_specs=in_specs, out_specs=out_specs, out_shape=out_shape,
        compiler_params=_cparams(("parallel", "parallel")),
        name="pre_attn_rope" if rope else "pre_attn_diff",
    )(*args)


def _attend(q_ref, k_ref, v_ref, qpad_scr, m_scr, acc_scr, *, n_items, k_tile, k_half, v_index, bias_tile,
            write_out):
    nc = v_ref.shape[2]
    tq = q_ref.shape[2]
    assert nc % CHUNKS_PER_BODY == 0
    zeros = jnp.zeros((HEAD_DIM, tq), BF16)
    for i in range(n_items):
        q = q_ref[0, i * HEAD_DIM:(i + 1) * HEAD_DIM, :]
        qpad_scr[i] = jnp.concatenate([q, zeros] if k_half(i) == 0 else [zeros, q], axis=0)

    def scores(j, i):
        t = k_tile(i)
        kc = k_ref[0, pl.ds(pl.multiple_of(j * KV_CHUNK, KV_CHUNK), KV_CHUNK), t * LANES:(t + 1) * LANES]
        s = jnp.dot(kc, qpad_scr[i], preferred_element_type=F32)
        b = bias_tile(i, j)
        return s if b is None else s + b

    def reset():
        m_scr[...] = jnp.full(m_scr.shape, NEG_BIG, F32)
        acc_scr[...] = jnp.zeros(acc_scr.shape, F32)

    reset()

    def fast_body(t, carry):
        seq = [(CHUNKS_PER_BODY * t + u, i) for i in range(n_items) for u in range(CHUNKS_PER_BODY)]
        pending = [scores(*seq[n]) for n in range(QK_LOOKAHEAD)]
        for n, (j, i) in enumerate(seq):
            s = pending[n]
            if n + QK_LOOKAHEAD < len(seq):
                pending.append(scores(*seq[n + QK_LOOKAHEAD]))
            m_tile = jnp.max(s.reshape(-1, SUBLANES, tq), axis=0)
            pv_tile = jnp.dot(v_ref[0, v_index(i), j], jnp.exp(s).astype(BF16), preferred_element_type=F32)
            first = n % CHUNKS_PER_BODY == 0
            m_part = m_tile if first else jnp.maximum(m_part, m_tile)
            pv = pv_tile if first else pv + pv_tile
            if n % CHUNKS_PER_BODY == CHUNKS_PER_BODY - 1:
                m_scr[i] = jnp.maximum(m_scr[i], m_part)
                acc_scr[i] += pv
        return carry

    lax.fori_loop(0, nc // CHUNKS_PER_BODY, fast_body, 0)
    l_min = write_out(acc_scr)
    in_range = jnp.logical_and(jnp.max(m_scr[...]) <= SCORE_GUARD, l_min >= DENOM_FLOOR)

    @pl.when(jnp.logical_not(in_range))
    def _():
        reset()

        def safe_chunk(j, carry):
            tiles = [scores(j, i) for i in range(n_items)]
            for i in range(n_items):
                s = tiles[i]
                m_old = m_scr[i, 0:1]
                m_new = jnp.maximum(m_old, jnp.max(s, axis=0, keepdims=True))
                p = jnp.exp(s - m_new).astype(BF16)
                pv = jnp.dot(v_ref[0, v_index(i), j], p, preferred_element_type=F32)
                acc_scr[i] = jnp.exp(m_old - m_new) * acc_scr[i] + pv
                m_scr[i, 0:1] = m_new
            return carry

        lax.fori_loop(0, nc, safe_chunk, 0)
        write_out(acc_scr)


def _gqa_attn_kernel(q_ref, k_ref, v_ref, o_ref, qpad_scr, m_scr, acc_scr, *, n_items, group):
    def write_out(acc_ref):
        l_min = jnp.full((1, 1), jnp.inf, F32)
        for i in range(n_items):
            acc = acc_ref[i]
            l = acc[HEAD_DIM:HEAD_DIM + 1]
            l_min = jnp.minimum(l_min, jnp.min(l, axis=1, keepdims=True))
            o_ref[0, i * HEAD_DIM:(i + 1) * HEAD_DIM, :] = (acc[:HEAD_DIM] / l).astype(BF16)
        return l_min[0, 0]

    _attend(q_ref, k_ref, v_ref, qpad_scr, m_scr, acc_scr, n_items=n_items,
            k_tile=lambda i: 0, k_half=lambda i: i // group, v_index=lambda i: i // group,
            bias_tile=lambda i, j: None, write_out=write_out)


def _gqa_attention(q_t, k, v_t):
    b, _, l = q_t.shape
    tq = Q_BLOCK
    group = A_HEADS // A_KV_HEADS
    n_items = 2 * group
    rows = n_items * HEAD_DIM
    nc = l // KV_CHUNK
    vr = HEAD_DIM + ONES_ROWS
    kern = functools.partial(_gqa_attn_kernel, n_items=n_items, group=group)
    return pl.pallas_call(
        kern,
        grid=(b, A_KV_HEADS // 2, l // tq),
        in_specs=[
            pl.BlockSpec((1, rows, tq), lambda bi, p, qi: (bi, p, qi)),
            pl.BlockSpec((1, l, LANES), lambda bi, p, qi: (bi, 0, p)),
            pl.BlockSpec((1, 2, nc, vr, KV_CHUNK), lambda bi, p, qi: (bi, p, 0, 0, 0)),
        ],
        out_specs=pl.BlockSpec((1, rows, tq), lambda bi, p, qi: (bi, p, qi)),
        out_shape=jax.ShapeDtypeStruct(q_t.shape, BF16),
        scratch_shapes=[
            pltpu.VMEM((n_items, 2 * HEAD_DIM, tq), BF16),
            pltpu.VMEM((n_items, SUBLANES, tq), F32),
            pltpu.VMEM((n_items, vr, tq), F32),
        ],
        compiler_params=_cparams(("parallel", "parallel", "arbitrary")),
        name="gqa_attention",
    )(q_t, k, v_t)


def _diff_attn_kernel(q_ref, k_ref, v_ref, bias_ref, lq1_ref, lk1_ref, lq2_ref, lk2_ref, sg_ref,
                      o_ref, qpad_scr, m_scr, acc_scr, *, lambda_init, n_heads):
    qi = pl.program_id(2)
    n_tiles = bias_ref.shape[1]

    def bias_tile(i, j):
        return bias_ref[i // 2, jnp.clip(j - qi * (Q_BLOCK // KV_CHUNK) + 2, 0, n_tiles - 1)]

    lam = (jnp.exp(jnp.sum(lq1_ref[...] * lk1_ref[...], axis=1, keepdims=True))
           - jnp.exp(jnp.sum(lq2_ref[...] * lk2_ref[...], axis=1, keepdims=True)) + lambda_init)

    def write_out(acc_ref):
        l_min = jnp.full((1, 1), jnp.inf, F32)
        for hh in range(n_heads):
            a0, a1 = acc_ref[2 * hh], acc_ref[2 * hh + 1]
            l0, l1 = a0[B_V_DIM:B_V_DIM + 1], a1[B_V_DIM:B_V_DIM + 1]
            l_min = jnp.minimum(l_min, jnp.min(jnp.minimum(l0, l1), axis=1, keepdims=True))
            o = a0[:B_V_DIM] / l0 - lam * (a1[:B_V_DIM] / l1)
            ms = jnp.mean(o * o, axis=0, keepdims=True)
            y = (o * lax.rsqrt(ms + EPS)) * sg_ref[...]
            o_ref[0, hh * B_V_DIM:(hh + 1) * B_V_DIM, :] = (y * (1.0 - lambda_init)).astype(BF16)
        return l_min[0, 0]

    _attend(q_ref, k_ref, v_ref, qpad_scr, m_scr, acc_scr, n_items=2 * n_heads,
            k_tile=lambda i: i // 2, k_half=lambda i: i % 2, v_index=lambda i: i // 2,
            bias_tile=bias_tile, write_out=write_out)


def _diff_attention(q_t, k, v_t, bias_tiles, lq1, lk1, lq2, lk2, subln_gain, *, lambda_init):
    b, _, l = q_t.shape
    tq = Q_BLOCK
    nh = DIFF_HEADS_PER_STEP
    nc = l // KV_CHUNK
    vr = B_V_DIM + ONES_ROWS
    n_tiles = bias_tiles.shape[1]
    rows = nh * 2 * HEAD_DIM
    kern = functools.partial(_diff_attn_kernel, lambda_init=lambda_init, n_heads=nh)
    vec = pl.BlockSpec((1, HEAD_DIM), lambda hp, bi, qi: (0, 0))
    return pl.pallas_call(
        kern,
        grid=(B_HEADS // nh, b, l // tq),
        in_specs=[
            pl.BlockSpec((1, rows, tq), lambda hp, bi, qi: (bi, hp, qi)),
            pl.BlockSpec((1, l, nh * LANES), lambda hp, bi, qi: (bi, 0, hp)),
            pl.BlockSpec((1, nh, nc, vr, KV_CHUNK), lambda hp, bi, qi: (bi, hp, 0, 0, 0)),
            pl.BlockSpec((nh, n_tiles, KV_CHUNK, tq), lambda hp, bi, qi: (hp, 0, 0, 0),
                         pipeline_mode=pl.Buffered(1)),
            vec, vec, vec, vec,
            pl.BlockSpec((B_V_DIM, 1), lambda hp, bi, qi: (0, 0)),
        ],
        out_specs=pl.BlockSpec((1, rows, tq), lambda hp, bi, qi: (bi, hp, qi)),
        out_shape=jax.ShapeDtypeStruct(q_t.shape, BF16),
        scratch_shapes=[
            pltpu.VMEM((2 * nh, 2 * HEAD_DIM, tq), BF16),
            pltpu.VMEM((2 * nh, SUBLANES, tq), F32),
            pltpu.VMEM((2 * nh, vr, tq), F32),
        ],
        compiler_params=_cparams(("parallel", "parallel", "arbitrary")),
        name="diff_attention",
    )(q_t, k, v_t, bias_tiles, lq1, lk1, lq2, lk2, subln_gain)


def _bias_tile_kernel(tbl_ref, bkt_ref, o_ref):
    h = pl.program_id(0)
    bkt = bkt_ref[0]
    acc = jnp.zeros(bkt.shape, F32)
    for b in range(NUM_BUCKETS):
        acc = jnp.where(bkt == b, tbl_ref[b, h], acc)
    o_ref[0, 0] = acc


def _bias_tiles(table, buckets):
    n_t, nk, nq = buckets.shape
    n_h = table.shape[1]
    grid_spec = pltpu.PrefetchScalarGridSpec(
        num_scalar_prefetch=1,
        grid=(n_h, n_t),
        in_specs=[pl.BlockSpec((1, nk, nq), lambda h, t, *_: (t, 0, 0))],
        out_specs=pl.BlockSpec((1, 1, nk, nq), lambda h, t, *_: (h, t, 0, 0)),
    )
    return pl.pallas_call(
        _bias_tile_kernel, grid_spec=grid_spec,
        out_shape=jax.ShapeDtypeStruct((n_h, n_t, nk, nq), F32),
        compiler_params=_cparams(("arbitrary", "arbitrary")),
        name="t5_bias_tiles",
    )(table, buckets)


def _post_kernel(a_ref, wo_ref, x_ref, mod_ref, g2_ref, wg_ref, wu_ref, wd_ref, o_ref, x1_scr, h_scr, acc_scr):
    j = pl.program_id(2)
    mod = mod_ref[0, 0]

    @pl.when(j == 0)
    def _():
        y = lax.dot_general(a_ref[0], wo_ref[...], (((0,), (0,)), ((), ())), preferred_element_type=F32)
        x1 = x_ref[0] + mod[2:3] * y
        x1_scr[...] = x1
        h_scr[...] = _rms_modulate(x1, g2_ref[0], mod[4:5], mod[3:4]).astype(BF16)
        acc_scr[...] = jnp.zeros(acc_scr.shape, F32)

    h = h_scr[...]
    g = jnp.dot(h, wg_ref[0], preferred_element_type=F32)
    u = jnp.dot(h, wu_ref[0], preferred_element_type=F32)
    a = ((g / (1.0 + jnp.exp(-g))) * u).astype(BF16)
    acc_scr[...] += jnp.dot(a, wd_ref[0], preferred_element_type=F32)

    @pl.when(j == pl.num_programs(2) - 1)
    def _():
        o_ref[0] = x1_scr[...] + mod[5:6] * acc_scr[...]


def _post_attention(attn_t, wo, x, mod, layer, seq_off, g2, w_gu, w_d, *, tm, n_chunks):
    b, l, d = x.shape
    fc = D_FF // n_chunks
    return pl.pallas_call(
        _post_kernel,
        grid=(b, l // tm, n_chunks),
        in_specs=[
            pl.BlockSpec((1, d, tm), lambda bi, ti, j: (bi, 0, ti)),
            pl.BlockSpec((d, d), lambda bi, ti, j: (0, 0)),
            pl.BlockSpec((1, tm, d), lambda bi, ti, j: (bi, ti, 0)),
            pl.BlockSpec((1, 1, N_MOD, d), lambda bi, ti, j: (layer, bi + seq_off, 0, 0)),
            pl.BlockSpec((1, 1, d), lambda bi, ti, j: (layer, 0, 0)),
            pl.BlockSpec((1, d, fc), lambda bi, ti, j: (layer, 0, j)),
            pl.BlockSpec((1, d, fc), lambda bi, ti, j: (layer, 0, j + n_chunks)),
            pl.BlockSpec((1, fc, d), lambda bi, ti, j: (layer, j, 0)),
        ],
        out_specs=pl.BlockSpec((1, tm, d), lambda bi, ti, j: (bi, ti, 0)),
        out_shape=jax.ShapeDtypeStruct(x.shape, F32),
        scratch_shapes=[pltpu.VMEM((tm, d), F32), pltpu.VMEM((tm, d), BF16), pltpu.VMEM((tm, d), F32)],
        compiler_params=_cparams(("parallel", "parallel", "arbitrary")),
        name="post_attn_ffn",
    )(attn_t, wo, x, mod, g2, w_gu, w_gu, w_d)


def _rope_tables(length):
    n_rows = length // GRID_W
    rr, cc = jnp.meshgrid(jnp.arange(n_rows), jnp.arange(GRID_W), indexing="ij")
    rows = rr.reshape(-1).astype(F32)
    cols = cc.reshape(-1).astype(F32)
    n_pairs = HEAD_DIM // 4
    inv_freq = ROPE_THETA ** (-jnp.arange(n_pairs, dtype=F32) / n_pairs)
    ang = jnp.concatenate([rows[:, None] * inv_freq[None], cols[:, None] * inv_freq[None]], axis=-1)
    return jnp.cos(ang).T, jnp.sin(ang).T


def _t5_bucket(rel):
    nb = NUM_BUCKETS // 2
    max_exact = nb // 2
    ret = (rel > 0).astype(jnp.int32) * nb
    n = jnp.abs(rel)
    large = max_exact + (jnp.log(jnp.maximum(n, 1).astype(F32) / max_exact)
                         / math.log(MAX_DISTANCE / max_exact) * (nb - max_exact)).astype(jnp.int32)
    large = jnp.minimum(large, nb - 1)
    return ret + jnp.where(n < max_exact, n, large)


def _bias_buckets():
    assert Q_BLOCK % KV_CHUNK == 0 and KV_CHUNK >= MAX_DISTANCE
    off = jnp.arange(-2, Q_BLOCK // KV_CHUNK + 2, dtype=jnp.int32)[:, None, None]
    kk = jnp.arange(KV_CHUNK, dtype=jnp.int32)[None, :, None]
    qq = jnp.arange(Q_BLOCK, dtype=jnp.int32)[None, None, :]
    return _t5_bucket(off * KV_CHUNK + kk - qq)


def _deinterleave(n_heads):
    base = jnp.concatenate([jnp.arange(0, HEAD_DIM, 2), jnp.arange(1, HEAD_DIM, 2)])
    return (jnp.arange(n_heads)[:, None] * HEAD_DIM + base[None, :]).reshape(-1)


def _col(v):
    return v.astype(F32).reshape(-1, 1)


def _trunk(x, mod, seq_off, p, *, tm):
    _, l, _ = x.shape
    cos_t, sin_t = _rope_tables(l)
    for layer in range(DEPTH):
        if layer % 2 == 0:
            q_t, k, v_t = _pre_attention(
                x, mod, layer, seq_off, p["g1"], p["a_wt"], p["a_gq"], p["a_gk"], cos_t, sin_t,
                n_q=A_HEADS, n_k=A_KV_HEADS, n_v=A_KV_HEADS, v_dim=HEAD_DIM, tm=tm)
            attn_t = _gqa_attention(q_t, k, v_t)
            wo = p["a_wo"]
        else:
            lambda_init = 0.8 - 0.6 * math.exp(-0.3 * layer)
            q_t, k, v_t = _pre_attention(
                x, mod, layer, seq_off, p["g1"], p["b_wt"], p["b_gq"], p["b_gk"], None, None,
                n_q=2 * B_HEADS, n_k=2 * B_HEADS, n_v=B_HEADS, v_dim=B_V_DIM, tm=tm)
            attn_t = _diff_attention(q_t, k, v_t, p["bias_tiles"], p["lq1"], p["lk1"], p["lq2"], p["lk2"],
                                     p["subln"], lambda_init=lambda_init)
            wo = p["b_wo"]
        x = _post_attention(attn_t, wo, x, mod, layer, seq_off, p["g2"], p["w_gu"], p["w_d"], tm=tm, n_chunks=2)
    return x


def kernel(x_prompt, x_sample, c_prompt, c_sample, norm1_gain, norm2_gain, w_ada, b_ada, w_gate_up, w_down, rel_bias, a_w_qkv, a_w_o, a_q_gain, a_k_gain, b_w_qkv, b_w_o, b_q_gain, b_k_gain, b_lq1, b_lk1, b_lq2, b_lk2, b_subln_gain):
    tm = 512
    n_prompt = c_prompt.shape[0]
    n_seq = n_prompt + c_sample.shape[0]
    pad = (-n_seq) % 8
    c_all = jnp.concatenate([c_prompt, c_sample, jnp.zeros((pad, D_MODEL), F32)], axis=0)
    mod = _modulation(c_all, w_ada, b_ada).reshape(DEPTH, n_seq + pad, N_MOD, D_MODEL)

    perm_q = _deinterleave(A_HEADS)
    perm_k = A_HEADS * HEAD_DIM + _deinterleave(A_KV_HEADS)
    v_cols = jnp.arange((A_HEADS + A_KV_HEADS) * HEAD_DIM, (A_HEADS + 2 * A_KV_HEADS) * HEAD_DIM)
    a_perm = jnp.concatenate([perm_q, perm_k, v_cols])
    head_perm = _deinterleave(1)
    p = {
        "g1": norm1_gain.reshape(DEPTH, 1, D_MODEL),
        "g2": norm2_gain.reshape(DEPTH, 1, D_MODEL),
        "a_wt": a_w_qkv[0][:, a_perm].T.astype(BF16),
        "a_gq": _col(a_q_gain[0][head_perm]),
        "a_gk": _col(a_k_gain[0][head_perm]),
        "a_wo": a_w_o[0].astype(BF16),
        "b_wt": b_w_qkv[0].T.astype(BF16),
        "b_gq": _col(b_q_gain[0]),
        "b_gk": _col(b_k_gain[0]),
        "b_wo": b_w_o[0].astype(BF16),
        "lq1": b_lq1.astype(F32), "lk1": b_lk1.astype(F32), "lq2": b_lq2.astype(F32), "lk2": b_lk2.astype(F32),
        "subln": _col(b_subln_gain[0]),
        "bias_tiles": _bias_tiles(rel_bias.astype(F32), _bias_buckets()),
        "w_gu": w_gate_up.astype(BF16),
        "w_d": w_down.astype(BF16),
    }
    y_prompt = _trunk(x_prompt, mod, 0, p, tm=tm)
    y_sample = _trunk(x_sample, mod, n_prompt, p, tm=tm)
    return (y_prompt, y_sample)
```

```python
import functools
import math

import jax
import jax.numpy as jnp
from jax import lax
from jax.experimental import pallas as pl
from jax.experimental.pallas import tpu as pltpu

F32 = jnp.float32
BF16 = jnp.bfloat16

D_MODEL = 1024
DEPTH = 2
GRID_W = 64
HEAD_DIM = 64
A_HEADS = 16
A_KV_HEADS = 4
B_HEADS = 8
B_V_DIM = 128
NUM_BUCKETS = 32
MAX_DISTANCE = 128
ROPE_THETA = 10000.0
D_FF = 2816
N_MOD = 6
EPS = 1e-6

LANES = 128
SUBLANES = 8
BF16_ROWS = 16
KV_CHUNK = 256
Q_BLOCK = 512
CHUNKS_PER_BODY = 4
FF_CHUNK = 256
ONES_ROWS = BF16_ROWS
DIFF_HEADS_PER_STEP = 2
QK_LOOKAHEAD = 2
NEG_BIG = -1e30
SCORE_GUARD = 30.0
DENOM_FLOOR = 1e-30
VMEM_LIMIT = 56 * 1024 * 1024


def _cparams(sem):
    return pltpu.CompilerParams(dimension_semantics=sem, vmem_limit_bytes=VMEM_LIMIT)


def _mod_kernel(c_ref, w_ref, b_ref, o_ref):
    c = c_ref[...]
    c_act = (c / (1.0 + jnp.exp(-c))).astype(BF16)
    w = w_ref[0].astype(BF16)
    o_ref[0] = jnp.dot(c_act, w, preferred_element_type=F32) + b_ref[0]


def _modulation(c_all, w_ada, b_ada):
    s = c_all.shape[0]
    tn = 1024
    n_out = N_MOD * D_MODEL
    return pl.pallas_call(
        _mod_kernel,
        grid=(DEPTH, n_out // tn),
        in_specs=[
            pl.BlockSpec((s, D_MODEL), lambda i, n: (0, 0)),
            pl.BlockSpec((1, D_MODEL, tn), lambda i, n: (i, 0, n)),
            pl.BlockSpec((1, 1, tn), lambda i, n: (i, 0, n)),
        ],
        out_specs=pl.BlockSpec((1, s, tn), lambda i, n: (i, 0, n)),
        out_shape=jax.ShapeDtypeStruct((DEPTH, s, n_out), F32),
        compiler_params=_cparams(("arbitrary", "arbitrary")),
        name="adaln_mod",
    )(c_all, w_ada, b_ada.reshape(DEPTH, 1, n_out))


def _rms_modulate(x, gain, scale, shift):
    ms = jnp.mean(x * x, axis=-1, keepdims=True)
    y = x * lax.rsqrt(ms + EPS)
    return (y * gain) * (1.0 + scale) + shift


def _pre_kernel(x_ref, mod_ref, g1_ref, wt_ref, gq_ref, gk_ref, *rest, n_q, n_k, n_v, v_dim, rope):
    if rope:
        cos_ref, sin_ref, q_ref, k_ref, v_ref = rest
    else:
        q_ref, k_ref, v_ref = rest
    tm = x_ref.shape[1]
    mod = mod_ref[0, 0]
    h = _rms_modulate(x_ref[0], g1_ref[0], mod[1:2], mod[0:1]).astype(BF16)
    qkv_t = lax.dot_general(wt_ref[...], h, (((1,), (1,)), ((), ())), preferred_element_type=F32)

    def head_norm(blk, gain):
        ms = jnp.mean(blk * blk, axis=0, keepdims=True)
        y = blk * lax.rsqrt(ms + EPS) * gain
        if rope:
            half = HEAD_DIM // 2
            x1, x2 = y[:half], y[half:]
            c, s = cos_ref[...], sin_ref[...]
            y = jnp.concatenate([x1 * c - x2 * s, x1 * s + x2 * c], axis=0)
        return y

    gq, gk = gq_ref[...], gk_ref[...]
    for i in range(n_q):
        blk = head_norm(qkv_t[i * HEAD_DIM:(i + 1) * HEAD_DIM], gq)
        q_ref[0, i * HEAD_DIM:(i + 1) * HEAD_DIM, :] = (blk * (HEAD_DIM ** -0.5)).astype(BF16)
    k_rows = n_q * HEAD_DIM
    k_t = jnp.concatenate(
        [head_norm(qkv_t[k_rows + i * HEAD_DIM:k_rows + (i + 1) * HEAD_DIM], gk) for i in range(n_k)], axis=0)
    k_ref[0] = k_t.T.astype(BF16)
    v_rows = k_rows + n_k * HEAD_DIM
    ones_blk = (lax.broadcasted_iota(jnp.int32, (ONES_ROWS, KV_CHUNK), 0) == 0).astype(BF16)
    for g in range(n_v):
        v_t = qkv_t[v_rows + g * v_dim:v_rows + (g + 1) * v_dim].astype(BF16)
        for c in range(tm // KV_CHUNK):
            v_ref[0, g, c, 0:v_dim, :] = v_t[:, c * KV_CHUNK:(c + 1) * KV_CHUNK]
            v_ref[0, g, c, v_dim:v_dim + ONES_ROWS, :] = ones_blk


def _pre_attention(x, mod, layer, seq_off, g1, wt, gq, gk, cos_t, sin_t, *, n_q, n_k, n_v, v_dim, tm):
    b, l, d = x.shape
    rope = cos_t is not None
    n_rows = wt.shape[0]
    nc = l // KV_CHUNK
    in_specs = [
        pl.BlockSpec((1, tm, d), lambda bi, ti: (bi, ti, 0)),
        pl.BlockSpec((1, 1, N_MOD, d), lambda bi, ti: (layer, bi + seq_off, 0, 0)),
        pl.BlockSpec((1, 1, d), lambda bi, ti: (layer, 0, 0)),
        pl.BlockSpec((n_rows, d), lambda bi, ti: (0, 0)),
        pl.BlockSpec((HEAD_DIM, 1), lambda bi, ti: (0, 0)),
        pl.BlockSpec((HEAD_DIM, 1), lambda bi, ti: (0, 0)),
    ]
    args = [x, mod, g1, wt, gq, gk]
    if rope:
        in_specs += [pl.BlockSpec((HEAD_DIM // 2, tm), lambda bi, ti: (0, ti))] * 2
        args += [cos_t, sin_t]
    vr = v_dim + ONES_ROWS
    out_shape = (
        jax.ShapeDtypeStruct((b, n_q * HEAD_DIM, l), BF16),
        jax.ShapeDtypeStruct((b, l, n_k * HEAD_DIM), BF16),
        jax.ShapeDtypeStruct((b, n_v, nc, vr, KV_CHUNK), BF16),
    )
    out_specs = (
        pl.BlockSpec((1, n_q * HEAD_DIM, tm), lambda bi, ti: (bi, 0, ti)),
        pl.BlockSpec((1, tm, n_k * HEAD_DIM), lambda bi, ti: (bi, ti, 0)),
        pl.BlockSpec((1, n_v, tm // KV_CHUNK, vr, KV_CHUNK), lambda bi, ti: (bi, 0, ti, 0, 0)),
    )
    kern = functools.partial(_pre_kernel, n_q=n_q, n_k=n_k, n_v=n_v, v_dim=v_dim, rope=rope)
    return pl.pallas_call(
        kern, grid=(b, l // tm), in_specs=in_specs, out_specs=out_specs, out_shape=out_shape,
        compiler_params=_cparams(("parallel", "parallel")),
        name="pre_attn_rope" if rope else "pre_attn_diff",
    )(*args)


def _attend(q_ref, k_ref, v_ref, qpad_scr, m_scr, acc_scr, *, n_items, k_tile, k_half, v_index, bias_tile,
            write_out, far=None):
    nc = v_ref.shape[2]
    tq = q_ref.shape[2]
    assert nc % CHUNKS_PER_BODY == 0
    n_trips = nc // CHUNKS_PER_BODY
    zeros = jnp.zeros((HEAD_DIM, tq), BF16)
    for i in range(n_items):
        q = q_ref[0, i * HEAD_DIM:(i + 1) * HEAD_DIM, :]
        qpad_scr[i] = jnp.concatenate([q, zeros] if k_half(i) == 0 else [zeros, q], axis=0)

    def scores(j, i, biased=True):
        t = k_tile(i)
        kc = k_ref[0, pl.ds(pl.multiple_of(j * KV_CHUNK, KV_CHUNK), KV_CHUNK), t * LANES:(t + 1) * LANES]
        s = jnp.dot(kc, qpad_scr[i], preferred_element_type=F32)
        b = bias_tile(i, j) if biased else None
        return s if b is None else s + b

    def reset(m_ref, acc_ref):
        m_ref[...] = jnp.full(m_ref.shape, NEG_BIG, F32)
        acc_ref[...] = jnp.zeros(acc_ref.shape, F32)

    def fast_body(biased, m_ref, acc_ref):
        def body(t, carry):
            seq = [(CHUNKS_PER_BODY * t + u, i) for i in range(n_items) for u in range(CHUNKS_PER_BODY)]
            pending = [scores(*seq[n], biased) for n in range(QK_LOOKAHEAD)]
            for n, (j, i) in enumerate(seq):
                s = pending[n]
                if n + QK_LOOKAHEAD < len(seq):
                    pending.append(scores(*seq[n + QK_LOOKAHEAD], biased))
                m_tile = jnp.max(s.reshape(-1, SUBLANES, tq), axis=0)
                pv_tile = jnp.dot(v_ref[0, v_index(i), j], jnp.exp(s).astype(BF16), preferred_element_type=F32)
                first = n % CHUNKS_PER_BODY == 0
                m_part = m_tile if first else jnp.maximum(m_part, m_tile)
                pv = pv_tile if first else pv + pv_tile
                if n % CHUNKS_PER_BODY == CHUNKS_PER_BODY - 1:
                    m_ref[i] = jnp.maximum(m_ref[i], m_part)
                    acc_ref[i] += pv
            return carry
        return body

    reset(m_scr, acc_scr)
    if far is None:
        lax.fori_loop(0, n_trips, fast_body(True, m_scr, acc_scr), 0)
    else:
        near_lo, near_hi, side_bias, m_far, acc_far = far
        reset(m_far, acc_far)
        lax.fori_loop(0, near_lo, fast_body(False, m_scr, acc_scr), 0)
        lax.fori_loop(near_hi, n_trips, fast_body(False, m_far, acc_far), 0)
        for i in range(n_items):
            c_left, c_right = side_bias(i)
            e_left = jnp.exp(jnp.full((1, 1), c_left, F32))
            e_right = jnp.exp(jnp.full((1, 1), c_right, F32))
            acc_scr[i] = e_left * acc_scr[i] + e_right * acc_far[i]
            m_scr[i] = jnp.maximum(m_scr[i] + c_left, m_far[i] + c_right)
        lax.fori_loop(near_lo, near_hi, fast_body(True, m_scr, acc_scr), 0)
    l_min = write_out(acc_scr)
    in_range = jnp.logical_and(jnp.max(m_scr[...]) <= SCORE_GUARD, l_min >= DENOM_FLOOR)

    @pl.when(jnp.logical_not(in_range))
    def _():
        reset(m_scr, acc_scr)

        def safe_chunk(j, carry):
            tiles = [scores(j, i) for i in range(n_items)]
            for i in range(n_items):
                s = tiles[i]
                m_old = m_scr[i, 0:1]
                m_new = jnp.maximum(m_old, jnp.max(s, axis=0, keepdims=True))
                p = jnp.exp(s - m_new).astype(BF16)
                pv = jnp.dot(v_ref[0, v_index(i), j], p, preferred_element_type=F32)
                acc_scr[i] = jnp.exp(m_old - m_new) * acc_scr[i] + pv
                m_scr[i, 0:1] = m_new
            return carry

        lax.fori_loop(0, nc, safe_chunk, 0)
        write_out(acc_scr)


def _gqa_attn_kernel(q_ref, k_ref, v_ref, o_ref, qpad_scr, m_scr, acc_scr, *, n_items, group):
    def write_out(acc_ref):
        l_min = jnp.full((1, 1), jnp.inf, F32)
        for i in range(n_items):
            acc = acc_ref[i]
            l = acc[HEAD_DIM:HEAD_DIM + 1]
            l_min = jnp.minimum(l_min, jnp.min(l, axis=1, keepdims=True))
            o_ref[0, i * HEAD_DIM:(i + 1) * HEAD_DIM, :] = (acc[:HEAD_DIM] / l).astype(BF16)
        return l_min[0, 0]

    _attend(q_ref, k_ref, v_ref, qpad_scr, m_scr, acc_scr, n_items=n_items,
            k_tile=lambda i: 0, k_half=lambda i: i // group, v_index=lambda i: i // group,
            bias_tile=lambda i, j: None, write_out=write_out)


def _gqa_attention(q_t, k, v_t):
    b, _, l = q_t.shape
    tq = Q_BLOCK
    group = A_HEADS // A_KV_HEADS
    n_items = 2 * group
    rows = n_items * HEAD_DIM
    nc = l // KV_CHUNK
    vr = HEAD_DIM + ONES_ROWS
    kern = functools.partial(_gqa_attn_kernel, n_items=n_items, group=group)
    return pl.pallas_call(
        kern,
        grid=(b, A_KV_HEADS // 2, l // tq),
        in_specs=[
            pl.BlockSpec((1, rows, tq), lambda bi, p, qi: (bi, p, qi)),
            pl.BlockSpec((1, l, LANES), lambda bi, p, qi: (bi, 0, p)),
            pl.BlockSpec((1, 2, nc, vr, KV_CHUNK), lambda bi, p, qi: (bi, p, 0, 0, 0)),
        ],
        out_specs=pl.BlockSpec((1, rows, tq), lambda bi, p, qi: (bi, p, qi)),
        out_shape=jax.ShapeDtypeStruct(q_t.shape, BF16),
        scratch_shapes=[
            pltpu.VMEM((n_items, 2 * HEAD_DIM, tq), BF16),
            pltpu.VMEM((n_items, SUBLANES, tq), F32),
            pltpu.VMEM((n_items, vr, tq), F32),
        ],
        compiler_params=_cparams(("parallel", "parallel", "arbitrary")),
        name="gqa_attention",
    )(q_t, k, v_t)


def _diff_attn_kernel(tbl_ref, q_ref, k_ref, v_ref, bias_ref, lq1_ref, lk1_ref, lq2_ref, lk2_ref, sg_ref,
                      o_ref, qpad_scr, m_scr, acc_scr, m_far, acc_far, *, lambda_init, n_heads):
    hp = pl.program_id(0)
    qi = pl.program_id(2)
    n_tiles = bias_ref.shape[1]
    q_per_k = Q_BLOCK // KV_CHUNK
    n_trips = v_ref.shape[2] // CHUNKS_PER_BODY

    def bias_tile(i, j):
        return bias_ref[i // 2, jnp.clip(j - qi * q_per_k + 2, 0, n_tiles - 1)]

    def side_bias(i):
        h = hp * n_heads + i // 2
        return tbl_ref[NUM_BUCKETS // 2 - 1, h], tbl_ref[NUM_BUCKETS - 1, h]

    near_lo = jnp.maximum(qi * q_per_k - 1, 0) // CHUNKS_PER_BODY
    near_hi = jnp.minimum((qi * q_per_k + q_per_k) // CHUNKS_PER_BODY + 1, n_trips)

    lam = (jnp.exp(jnp.sum(lq1_ref[...] * lk1_ref[...], axis=1, keepdims=True))
           - jnp.exp(jnp.sum(lq2_ref[...] * lk2_ref[...], axis=1, keepdims=True)) + lambda_init)

    def write_out(acc_ref):
        l_min = jnp.full((1, 1), jnp.inf, F32)
        for hh in range(n_heads):
            a0, a1 = acc_ref[2 * hh], acc_ref[2 * hh + 1]
            l0, l1 = a0[B_V_DIM:B_V_DIM + 1], a1[B_V_DIM:B_V_DIM + 1]
            l_min = jnp.minimum(l_min, jnp.min(jnp.minimum(l0, l1), axis=1, keepdims=True))
            o = a0[:B_V_DIM] / l0 - lam * (a1[:B_V_DIM] / l1)
            ms = jnp.mean(o * o, axis=0, keepdims=True)
            y = (o * lax.rsqrt(ms + EPS)) * sg_ref[...]
            o_ref[0, hh * B_V_DIM:(hh + 1) * B_V_DIM, :] = (y * (1.0 - lambda_init)).astype(BF16)
        return l_min[0, 0]

    _attend(q_ref, k_ref, v_ref, qpad_scr, m_scr, acc_scr, n_items=2 * n_heads,
            k_tile=lambda i: i // 2, k_half=lambda i: i % 2, v_index=lambda i: i // 2,
            bias_tile=bias_tile, write_out=write_out, far=(near_lo, near_hi, side_bias, m_far, acc_far))


def _diff_attention(q_t, k, v_t, bias_tiles, table, lq1, lk1, lq2, lk2, subln_gain, *, lambda_init):
    b, _, l = q_t.shape
    tq = Q_BLOCK
    nh = DIFF_HEADS_PER_STEP
    nc = l // KV_CHUNK
    vr = B_V_DIM + ONES_ROWS
    n_tiles = bias_tiles.shape[1]
    rows = nh * 2 * HEAD_DIM
    kern = functools.partial(_diff_attn_kernel, lambda_init=lambda_init, n_heads=nh)
    vec = pl.BlockSpec((1, HEAD_DIM), lambda hp, bi, qi, *_: (0, 0))
    grid_spec = pltpu.PrefetchScalarGridSpec(
        num_scalar_prefetch=1,
        grid=(B_HEADS // nh, b, l // tq),
        in_specs=[
            pl.BlockSpec((1, rows, tq), lambda hp, bi, qi, *_: (bi, hp, qi)),
            pl.BlockSpec((1, l, nh * LANES), lambda hp, bi, qi, *_: (bi, 0, hp)),
            pl.BlockSpec((1, nh, nc, vr, KV_CHUNK), lambda hp, bi, qi, *_: (bi, hp, 0, 0, 0)),
            pl.BlockSpec((nh, n_tiles, KV_CHUNK, tq), lambda hp, bi, qi, *_: (hp, 0, 0, 0),
                         pipeline_mode=pl.Buffered(1)),
            vec, vec, vec, vec,
            pl.BlockSpec((B_V_DIM, 1), lambda hp, bi, qi, *_: (0, 0)),
        ],
        out_specs=pl.BlockSpec((1, rows, tq), lambda hp, bi, qi, *_: (bi, hp, qi)),
        scratch_shapes=[
            pltpu.VMEM((2 * nh, 2 * HEAD_DIM, tq), BF16),
            pltpu.VMEM((2 * nh, SUBLANES, tq), F32),
            pltpu.VMEM((2 * nh, vr, tq), F32),
            pltpu.VMEM((2 * nh, SUBLANES, tq), F32),
            pltpu.VMEM((2 * nh, vr, tq), F32),
        ],
    )
    return pl.pallas_call(
        kern, grid_spec=grid_spec,
        out_shape=jax.ShapeDtypeStruct(q_t.shape, BF16),
        compiler_params=_cparams(("parallel", "parallel", "arbitrary")),
        name="diff_attention",
    )(table, q_t, k, v_t, bias_tiles, lq1, lk1, lq2, lk2, subln_gain)


def _bias_tile_kernel(tbl_ref, bkt_ref, o_ref):
    h = pl.program_id(0)
    bkt = bkt_ref[0]
    acc = jnp.zeros(bkt.shape, F32)
    for b in range(NUM_BUCKETS):
        acc = jnp.where(bkt == b, tbl_ref[b, h], acc)
    o_ref[0, 0] = acc


def _bias_tiles(table, buckets):
    n_t, nk, nq = buckets.shape
    n_h = table.shape[1]
    grid_spec = pltpu.PrefetchScalarGridSpec(
        num_scalar_prefetch=1,
        grid=(n_h, n_t),
        in_specs=[pl.BlockSpec((1, nk, nq), lambda h, t, *_: (t, 0, 0))],
        out_specs=pl.BlockSpec((1, 1, nk, nq), lambda h, t, *_: (h, t, 0, 0)),
    )
    return pl.pallas_call(
        _bias_tile_kernel, grid_spec=grid_spec,
        out_shape=jax.ShapeDtypeStruct((n_h, n_t, nk, nq), F32),
        compiler_params=_cparams(("arbitrary", "arbitrary")),
        name="t5_bias_tiles",
    )(table, buckets)


def _post_kernel(a_ref, wo_ref, x_ref, mod_ref, g2_ref, wgu_ref, wd_ref, o_ref, act_scr):
    mod = mod_ref[0, 0]
    y = lax.dot_general(a_ref[0], wo_ref[...], (((0,), (0,)), ((), ())), preferred_element_type=F32)
    x1 = x_ref[0] + mod[2:3] * y
    h = _rms_modulate(x1, g2_ref[0], mod[4:5], mod[3:4]).astype(BF16)
    for c in range(D_FF // FF_CHUNK):
        gu = jnp.dot(h, wgu_ref[0, c], preferred_element_type=F32)
        g, u = gu[:, :FF_CHUNK], gu[:, FF_CHUNK:]
        act_scr[:, c * FF_CHUNK:(c + 1) * FF_CHUNK] = ((g / (1.0 + jnp.exp(-g))) * u).astype(BF16)
    ffn = jnp.dot(act_scr[...], wd_ref[0], preferred_element_type=F32)
    o_ref[0] = x1 + mod[5:6] * ffn


def _post_attention(attn_t, wo, x, mod, layer, seq_off, g2, w_gu, w_d, *, tm):
    b, l, d = x.shape
    once = pl.Buffered(1)
    return pl.pallas_call(
        _post_kernel,
        grid=(b, l // tm),
        in_specs=[
            pl.BlockSpec((1, d, tm), lambda bi, ti: (bi, 0, ti)),
            pl.BlockSpec((d, d), lambda bi, ti: (0, 0), pipeline_mode=once),
            pl.BlockSpec((1, tm, d), lambda bi, ti: (bi, ti, 0)),
            pl.BlockSpec((1, 1, N_MOD, d), lambda bi, ti: (layer, bi + seq_off, 0, 0)),
            pl.BlockSpec((1, 1, d), lambda bi, ti: (layer, 0, 0)),
            pl.BlockSpec((1, D_FF // FF_CHUNK, d, 2 * FF_CHUNK), lambda bi, ti: (layer, 0, 0, 0), pipeline_mode=once),
            pl.BlockSpec((1, D_FF, d), lambda bi, ti: (layer, 0, 0), pipeline_mode=once),
        ],
        out_specs=pl.BlockSpec((1, tm, d), lambda bi, ti: (bi, ti, 0)),
        out_shape=jax.ShapeDtypeStruct(x.shape, F32),
        scratch_shapes=[pltpu.VMEM((tm, D_FF), BF16)],
        compiler_params=_cparams(("parallel", "parallel")),
        name="post_attn_ffn",
    )(attn_t, wo, x, mod, g2, w_gu, w_d)


def _rope_tables(length):
    n_rows = length // GRID_W
    rr, cc = jnp.meshgrid(jnp.arange(n_rows), jnp.arange(GRID_W), indexing="ij")
    rows = rr.reshape(-1).astype(F32)
    cols = cc.reshape(-1).astype(F32)
    n_pairs = HEAD_DIM // 4
    inv_freq = ROPE_THETA ** (-jnp.arange(n_pairs, dtype=F32) / n_pairs)
    ang = jnp.concatenate([rows[:, None] * inv_freq[None], cols[:, None] * inv_freq[None]], axis=-1)
    return jnp.cos(ang).T, jnp.sin(ang).T


def _t5_bucket(rel):
    nb = NUM_BUCKETS // 2
    max_exact = nb // 2
    ret = (rel > 0).astype(jnp.int32) * nb
    n = jnp.abs(rel)
    large = max_exact + (jnp.log(jnp.maximum(n, 1).astype(F32) / max_exact)
                         / math.log(MAX_DISTANCE / max_exact) * (nb - max_exact)).astype(jnp.int32)
    large = jnp.minimum(large, nb - 1)
    return ret + jnp.where(n < max_exact, n, large)


def _bias_buckets():
    assert Q_BLOCK % KV_CHUNK == 0 and KV_CHUNK >= MAX_DISTANCE
    off = jnp.arange(-2, Q_BLOCK // KV_CHUNK + 2, dtype=jnp.int32)[:, None, None]
    kk = jnp.arange(KV_CHUNK, dtype=jnp.int32)[None, :, None]
    qq = jnp.arange(Q_BLOCK, dtype=jnp.int32)[None, None, :]
    return _t5_bucket(off * KV_CHUNK + kk - qq)


def _deinterleave(n_heads):
    base = jnp.concatenate([jnp.arange(0, HEAD_DIM, 2), jnp.arange(1, HEAD_DIM, 2)])
    return (jnp.arange(n_heads)[:, None] * HEAD_DIM + base[None, :]).reshape(-1)


def _col(v):
    return v.astype(F32).reshape(-1, 1)


def _chunk_gate_up(w_gate_up):
    n = D_FF // FF_CHUNK
    gate = w_gate_up[:, :, :D_FF].reshape(DEPTH, D_MODEL, n, FF_CHUNK)
    up = w_gate_up[:, :, D_FF:].reshape(DEPTH, D_MODEL, n, FF_CHUNK)
    return jnp.transpose(jnp.concatenate([gate, up], axis=-1), (0, 2, 1, 3)).astype(BF16)


def _trunk(x, mod, seq_off, p, *, tm):
    _, l, _ = x.shape
    cos_t, sin_t = _rope_tables(l)
    for layer in range(DEPTH):
        if layer % 2 == 0:
            q_t, k, v_t = _pre_attention(
                x, mod, layer, seq_off, p["g1"], p["a_wt"], p["a_gq"], p["a_gk"], cos_t, sin_t,
                n_q=A_HEADS, n_k=A_KV_HEADS, n_v=A_KV_HEADS, v_dim=HEAD_DIM, tm=tm)
            attn_t = _gqa_attention(q_t, k, v_t)
            wo = p["a_wo"]
        else:
            lambda_init = 0.8 - 0.6 * math.exp(-0.3 * layer)
            q_t, k, v_t = _pre_attention(
                x, mod, layer, seq_off, p["g1"], p["b_wt"], p["b_gq"], p["b_gk"], None, None,
                n_q=2 * B_HEADS, n_k=2 * B_HEADS, n_v=B_HEADS, v_dim=B_V_DIM, tm=tm)
            attn_t = _diff_attention(q_t, k, v_t, p["bias_tiles"], p["table"], p["lq1"], p["lk1"], p["lq2"],
                                     p["lk2"], p["subln"], lambda_init=lambda_init)
            wo = p["b_wo"]
        x = _post_attention(attn_t, wo, x, mod, layer, seq_off, p["g2"], p["w_gu"], p["w_d"], tm=tm)
    return x


def kernel(x_prompt, x_sample, c_prompt, c_sample, norm1_gain, norm2_gain, w_ada, b_ada, w_gate_up, w_down, rel_bias, a_w_qkv, a_w_o, a_q_gain, a_k_gain, b_w_qkv, b_w_o, b_q_gain, b_k_gain, b_lq1, b_lk1, b_lq2, b_lk2, b_subln_gain):
    tm = 512
    n_prompt = c_prompt.shape[0]
    n_seq = n_prompt + c_sample.shape[0]
    pad = (-n_seq) % 8
    c_all = jnp.concatenate([c_prompt, c_sample, jnp.zeros((pad, D_MODEL), F32)], axis=0)
    mod = _modulation(c_all, w_ada, b_ada).reshape(DEPTH, n_seq + pad, N_MOD, D_MODEL)

    perm_q = _deinterleave(A_HEADS)
    perm_k = A_HEADS * HEAD_DIM + _deinterleave(A_KV_HEADS)
    v_cols = jnp.arange((A_HEADS + A_KV_HEADS) * HEAD_DIM, (A_HEADS + 2 * A_KV_HEADS) * HEAD_DIM)
    a_perm = jnp.concatenate([perm_q, perm_k, v_cols])
    head_perm = _deinterleave(1)
    table = rel_bias.astype(F32)
    p = {
        "g1": norm1_gain.reshape(DEPTH, 1, D_MODEL),
        "g2": norm2_gain.reshape(DEPTH, 1, D_MODEL),
        "a_wt": a_w_qkv[0][:, a_perm].T.astype(BF16),
        "a_gq": _col(a_q_gain[0][head_perm]),
        "a_gk": _col(a_k_gain[0][head_perm]),
        "a_wo": a_w_o[0].astype(BF16),
        "b_wt": b_w_qkv[0].T.astype(BF16),
        "b_gq": _col(b_q_gain[0]),
        "b_gk": _col(b_k_gain[0]),
        "b_wo": b_w_o[0].astype(BF16),
        "lq1": b_lq1.astype(F32), "lk1": b_lk1.astype(F32), "lq2": b_lq2.astype(F32), "lk2": b_lk2.astype(F32),
        "subln": _col(b_subln_gain[0]),
        "table": table,
        "bias_tiles": _bias_tiles(table, _bias_buckets()),
        "w_gu": _chunk_gate_up(w_gate_up),
        "w_d": w_down.astype(BF16),
    }
    y_prompt = _trunk(x_prompt, mod, 0, p, tm=tm)
    y_sample = _trunk(x_sample, mod, n_prompt, p, tm=tm)
    return (y_prompt, y_sample)
```

```python
import functools
import math

import jax
import jax.numpy as jnp
from jax import lax
from jax.experimental import pallas as pl
from jax.experimental.pallas import tpu as pltpu

F32 = jnp.float32
BF16 = jnp.bfloat16

D_MODEL = 1024
DEPTH = 2
GRID_W = 64
HEAD_DIM = 64
A_HEADS = 16
A_KV_HEADS = 4
B_HEADS = 8
B_V_DIM = 128
NUM_BUCKETS = 32
MAX_DISTANCE = 128
ROPE_THETA = 10000.0
D_FF = 2816
N_MOD = 6
EPS = 1e-6

LANES = 128
SUBLANES = 8
BF16_ROWS = 16
KV_CHUNK = 256
Q_BLOCK = 512
CHUNKS_PER_BODY = 4
FF_CHUNK = 256
DIFF_HEADS_PER_STEP = 2
QK_LOOKAHEAD = 2
NEG_BIG = -1e30
LOG2_E = math.log2(math.e)
SCORE_GUARD = 30.0 * LOG2_E
DENOM_FLOOR = 1e-30
VMEM_LIMIT = 56 * 1024 * 1024


def _cparams(sem):
    return pltpu.CompilerParams(dimension_semantics=sem, vmem_limit_bytes=VMEM_LIMIT)


def _mod_kernel(c_ref, w_ref, b_ref, o_ref):
    c = c_ref[...]
    c_act = (c / (1.0 + jnp.exp(-c))).astype(BF16)
    w = w_ref[0].astype(BF16)
    o_ref[0] = jnp.dot(c_act, w, preferred_element_type=F32) + b_ref[0]


def _modulation(c_all, w_ada, b_ada):
    s = c_all.shape[0]
    tn = 1024
    n_out = N_MOD * D_MODEL
    return pl.pallas_call(
        _mod_kernel,
        grid=(DEPTH, n_out // tn),
        in_specs=[
            pl.BlockSpec((s, D_MODEL), lambda i, n: (0, 0)),
            pl.BlockSpec((1, D_MODEL, tn), lambda i, n: (i, 0, n)),
            pl.BlockSpec((1, 1, tn), lambda i, n: (i, 0, n)),
        ],
        out_specs=pl.BlockSpec((1, s, tn), lambda i, n: (i, 0, n)),
        out_shape=jax.ShapeDtypeStruct((DEPTH, s, n_out), F32),
        compiler_params=_cparams(("arbitrary", "arbitrary")),
        name="adaln_mod",
    )(c_all, w_ada, b_ada.reshape(DEPTH, 1, n_out))


def _rms_modulate(x, gain, scale, shift):
    ms = jnp.mean(x * x, axis=-1, keepdims=True)
    y = x * lax.rsqrt(ms + EPS)
    return (y * gain) * (1.0 + scale) + shift


def _pre_kernel(x_ref, mod_ref, g1_ref, wt_ref, gq_ref, gk_ref, *rest, n_q, n_k, n_v, v_dim, rope):
    if rope:
        cos_ref, sin_ref, q_ref, k_ref, v_ref = rest
    else:
        q_ref, k_ref, v_ref = rest
    tm = x_ref.shape[1]
    mod = mod_ref[0, 0]
    h = _rms_modulate(x_ref[0], g1_ref[0], mod[1:2], mod[0:1]).astype(BF16)
    qkv_t = lax.dot_general(wt_ref[...], h, (((1,), (1,)), ((), ())), preferred_element_type=F32)

    def head_norm(blk, gain):
        ms = jnp.mean(blk * blk, axis=0, keepdims=True)
        y = blk * lax.rsqrt(ms + EPS) * gain
        if rope:
            half = HEAD_DIM // 2
            x1, x2 = y[:half], y[half:]
            c, s = cos_ref[...], sin_ref[...]
            y = jnp.concatenate([x1 * c - x2 * s, x1 * s + x2 * c], axis=0)
        return y

    gq, gk = gq_ref[...], gk_ref[...]
    for i in range(n_q):
        blk = head_norm(qkv_t[i * HEAD_DIM:(i + 1) * HEAD_DIM], gq)
        q_ref[0, i * HEAD_DIM:(i + 1) * HEAD_DIM, :] = (blk * (HEAD_DIM ** -0.5 * LOG2_E)).astype(BF16)
    k_rows = n_q * HEAD_DIM
    k_t = jnp.concatenate(
        [head_norm(qkv_t[k_rows + i * HEAD_DIM:k_rows + (i + 1) * HEAD_DIM], gk) for i in range(n_k)], axis=0)
    k_ref[0] = k_t.T.astype(BF16)
    v_rows = k_rows + n_k * HEAD_DIM
    for g in range(n_v):
        v_t = qkv_t[v_rows + g * v_dim:v_rows + (g + 1) * v_dim].astype(BF16)
        for c in range(tm // KV_CHUNK):
            v_ref[0, g, c] = v_t[:, c * KV_CHUNK:(c + 1) * KV_CHUNK]


def _pre_attention(x, mod, layer, seq_off, g1, wt, gq, gk, cos_t, sin_t, *, n_q, n_k, n_v, v_dim, tm):
    b, l, d = x.shape
    rope = cos_t is not None
    n_rows = wt.shape[0]
    nc = l // KV_CHUNK
    in_specs = [
        pl.BlockSpec((1, tm, d), lambda bi, ti: (bi, ti, 0)),
        pl.BlockSpec((1, 1, N_MOD, d), lambda bi, ti: (layer, bi + seq_off, 0, 0)),
        pl.BlockSpec((1, 1, d), lambda bi, ti: (layer, 0, 0)),
        pl.BlockSpec((n_rows, d), lambda bi, ti: (0, 0)),
        pl.BlockSpec((HEAD_DIM, 1), lambda bi, ti: (0, 0)),
        pl.BlockSpec((HEAD_DIM, 1), lambda bi, ti: (0, 0)),
    ]
    args = [x, mod, g1, wt, gq, gk]
    if rope:
        in_specs += [pl.BlockSpec((HEAD_DIM // 2, tm), lambda bi, ti: (0, ti))] * 2
        args += [cos_t, sin_t]
    out_shape = (
        jax.ShapeDtypeStruct((b, n_q * HEAD_DIM, l), BF16),
        jax.ShapeDtypeStruct((b, l, n_k * HEAD_DIM), BF16),
        jax.ShapeDtypeStruct((b, n_v, nc, v_dim, KV_CHUNK), BF16),
    )
    out_specs = (
        pl.BlockSpec((1, n_q * HEAD_DIM, tm), lambda bi, ti: (bi, 0, ti)),
        pl.BlockSpec((1, tm, n_k * HEAD_DIM), lambda bi, ti: (bi, ti, 0)),
        pl.BlockSpec((1, n_v, tm // KV_CHUNK, v_dim, KV_CHUNK), lambda bi, ti: (bi, 0, ti, 0, 0)),
    )
    kern = functools.partial(_pre_kernel, n_q=n_q, n_k=n_k, n_v=n_v, v_dim=v_dim, rope=rope)
    return pl.pallas_call(
        kern, grid=(b, l // tm), in_specs=in_specs, out_specs=out_specs, out_shape=out_shape,
        compiler_params=_cparams(("parallel", "parallel")),
        name="pre_attn_rope" if rope else "pre_attn_diff",
    )(*args)


def _attend(q_ref, k_ref, v_ref, qpad_scr, m_scr, l_scr, acc_scr, *, n_items, k_tile, k_half, v_index,
            bias_tile, write_out, far=None):
    nc = v_ref.shape[2]
    tq = q_ref.shape[2]
    assert nc % CHUNKS_PER_BODY == 0
    n_trips = nc // CHUNKS_PER_BODY
    zeros = jnp.zeros((HEAD_DIM, tq), BF16)
    for i in range(n_items):
        q = q_ref[0, i * HEAD_DIM:(i + 1) * HEAD_DIM, :]
        qpad_scr[i] = jnp.concatenate([q, zeros] if k_half(i) == 0 else [zeros, q], axis=0)

    def scores(j, i, biased=True):
        t = k_tile(i)
        kc = k_ref[0, pl.ds(pl.multiple_of(j * KV_CHUNK, KV_CHUNK), KV_CHUNK), t * LANES:(t + 1) * LANES]
        s = jnp.dot(kc, qpad_scr[i], preferred_element_type=F32)
        b = bias_tile(i, j) if biased else None
        return s if b is None else s + b

    def reset(m_ref, l_ref, acc_ref):
        m_ref[...] = jnp.full(m_ref.shape, NEG_BIG, F32)
        l_ref[...] = jnp.zeros(l_ref.shape, F32)
        acc_ref[...] = jnp.zeros(acc_ref.shape, F32)

    def fast_body(biased, m_ref, l_ref, acc_ref):
        def body(t, carry):
            seq = [(CHUNKS_PER_BODY * t + u, i) for i in range(n_items) for u in range(CHUNKS_PER_BODY)]
            pending = [scores(*seq[n], biased) for n in range(QK_LOOKAHEAD)]
            for n, (j, i) in enumerate(seq):
                s = pending[n]
                if n + QK_LOOKAHEAD < len(seq):
                    pending.append(scores(*seq[n + QK_LOOKAHEAD], biased))
                p = jnp.exp2(s)
                m_tile = jnp.max(s.reshape(-1, SUBLANES, tq), axis=0)
                l_tile = jnp.sum(p.reshape(-1, SUBLANES, tq), axis=0)
                pv_tile = jnp.dot(v_ref[0, v_index(i), j], p.astype(BF16), preferred_element_type=F32)
                first = n % CHUNKS_PER_BODY == 0
                m_part = m_tile if first else jnp.maximum(m_part, m_tile)
                l_part = l_tile if first else l_part + l_tile
                pv = pv_tile if first else pv + pv_tile
                if n % CHUNKS_PER_BODY == CHUNKS_PER_BODY - 1:
                    m_ref[i] = jnp.maximum(m_ref[i], m_part)
                    l_ref[i] += l_part
                    acc_ref[i] += pv
            return carry
        return body

    reset(m_scr, l_scr, acc_scr)
    if far is None:
        lax.fori_loop(0, n_trips, fast_body(True, m_scr, l_scr, acc_scr), 0)
    else:
        near_lo, near_hi, side_bias, m_far, l_far, acc_far = far
        reset(m_far, l_far, acc_far)
        lax.fori_loop(0, near_lo, fast_body(False, m_scr, l_scr, acc_scr), 0)
        lax.fori_loop(near_hi, n_trips, fast_body(False, m_far, l_far, acc_far), 0)
        for i in range(n_items):
            c_left, c_right = side_bias(i)
            e_left = jnp.exp2(jnp.full((1, 1), c_left, F32))
            e_right = jnp.exp2(jnp.full((1, 1), c_right, F32))
            acc_scr[i] = e_left * acc_scr[i] + e_right * acc_far[i]
            l_scr[i] = e_left * l_scr[i] + e_right * l_far[i]
            m_scr[i] = jnp.maximum(m_scr[i] + c_left, m_far[i] + c_right)
        lax.fori_loop(near_lo, near_hi, fast_body(True, m_scr, l_scr, acc_scr), 0)
    l_min = write_out(acc_scr, l_scr)
    in_range = jnp.logical_and(jnp.max(m_scr[...]) <= SCORE_GUARD, l_min >= DENOM_FLOOR)

    @pl.when(jnp.logical_not(in_range))
    def _():
        reset(m_scr, l_scr, acc_scr)

        def safe_chunk(j, carry):
            tiles = [scores(j, i) for i in range(n_items)]
            for i in range(n_items):
                s = tiles[i]
                m_old = m_scr[i, 0:1]
                m_new = jnp.maximum(m_old, jnp.max(s, axis=0, keepdims=True))
                alpha = jnp.exp2(m_old - m_new)
                p = jnp.exp2(s - m_new)
                pv = jnp.dot(v_ref[0, v_index(i), j], p.astype(BF16), preferred_element_type=F32)
                acc_scr[i] = alpha * acc_scr[i] + pv
                l_scr[i, 0:1] = alpha * l_scr[i, 0:1] + jnp.sum(p, axis=0, keepdims=True)
                m_scr[i, 0:1] = m_new
            return carry

        lax.fori_loop(0, nc, safe_chunk, 0)
        write_out(acc_scr, l_scr)


def _gqa_attn_kernel(q_ref, k_ref, v_ref, o_ref, qpad_scr, m_scr, l_scr, acc_scr, *, n_items, group):
    def write_out(acc_ref, l_ref):
        l_min = jnp.full((1, 1), jnp.inf, F32)
        for i in range(n_items):
            l = jnp.sum(l_ref[i], axis=0, keepdims=True)
            l_min = jnp.minimum(l_min, jnp.min(l, axis=1, keepdims=True))
            o_ref[0, i * HEAD_DIM:(i + 1) * HEAD_DIM, :] = (acc_ref[i] / l).astype(BF16)
        return l_min[0, 0]

    _attend(q_ref, k_ref, v_ref, qpad_scr, m_scr, l_scr, acc_scr, n_items=n_items,
            k_tile=lambda i: 0, k_half=lambda i: i // group, v_index=lambda i: i // group,
            bias_tile=lambda i, j: None, write_out=write_out)


def _gqa_attention(q_t, k, v_t):
    b, _, l = q_t.shape
    tq = Q_BLOCK
    group = A_HEADS // A_KV_HEADS
    n_items = 2 * group
    rows = n_items * HEAD_DIM
    nc = l // KV_CHUNK
    kern = functools.partial(_gqa_attn_kernel, n_items=n_items, group=group)
    return pl.pallas_call(
        kern,
        grid=(b, A_KV_HEADS // 2, l // tq),
        in_specs=[
            pl.BlockSpec((1, rows, tq), lambda bi, p, qi: (bi, p, qi)),
            pl.BlockSpec((1, l, LANES), lambda bi, p, qi: (bi, 0, p)),
            pl.BlockSpec((1, 2, nc, HEAD_DIM, KV_CHUNK), lambda bi, p, qi: (bi, p, 0, 0, 0)),
        ],
        out_specs=pl.BlockSpec((1, rows, tq), lambda bi, p, qi: (bi, p, qi)),
        out_shape=jax.ShapeDtypeStruct(q_t.shape, BF16),
        scratch_shapes=[
            pltpu.VMEM((n_items, 2 * HEAD_DIM, tq), BF16),
            pltpu.VMEM((n_items, SUBLANES, tq), F32),
            pltpu.VMEM((n_items, SUBLANES, tq), F32),
            pltpu.VMEM((n_items, HEAD_DIM, tq), F32),
        ],
        compiler_params=_cparams(("parallel", "parallel", "arbitrary")),
        name="gqa_attention",
    )(q_t, k, v_t)


def _diff_attn_kernel(tbl_ref, q_ref, k_ref, v_ref, bias_ref, lq1_ref, lk1_ref, lq2_ref, lk2_ref, sg_ref,
                      o_ref, qpad_scr, m_scr, l_scr, acc_scr, m_far, l_far, acc_far, *, lambda_init, n_heads):
    hp = pl.program_id(0)
    qi = pl.program_id(2)
    n_tiles = bias_ref.shape[1]
    q_per_k = Q_BLOCK // KV_CHUNK
    n_trips = v_ref.shape[2] // CHUNKS_PER_BODY

    def bias_tile(i, j):
        return bias_ref[i // 2, jnp.clip(j - qi * q_per_k + 2, 0, n_tiles - 1)]

    def side_bias(i):
        h = hp * n_heads + i // 2
        return tbl_ref[NUM_BUCKETS // 2 - 1, h], tbl_ref[NUM_BUCKETS - 1, h]

    near_lo = jnp.maximum(qi * q_per_k - 1, 0) // CHUNKS_PER_BODY
    near_hi = jnp.minimum((qi * q_per_k + q_per_k) // CHUNKS_PER_BODY + 1, n_trips)

    lam = (jnp.exp(jnp.sum(lq1_ref[...] * lk1_ref[...], axis=1, keepdims=True))
           - jnp.exp(jnp.sum(lq2_ref[...] * lk2_ref[...], axis=1, keepdims=True)) + lambda_init)

    def write_out(acc_ref, l_ref):
        l_min = jnp.full((1, 1), jnp.inf, F32)
        for hh in range(n_heads):
            l0 = jnp.sum(l_ref[2 * hh], axis=0, keepdims=True)
            l1 = jnp.sum(l_ref[2 * hh + 1], axis=0, keepdims=True)
            l_min = jnp.minimum(l_min, jnp.min(jnp.minimum(l0, l1), axis=1, keepdims=True))
            o = acc_ref[2 * hh] / l0 - lam * (acc_ref[2 * hh + 1] / l1)
            ms = jnp.mean(o * o, axis=0, keepdims=True)
            y = (o * lax.rsqrt(ms + EPS)) * sg_ref[...]
            o_ref[0, hh * B_V_DIM:(hh + 1) * B_V_DIM, :] = (y * (1.0 - lambda_init)).astype(BF16)
        return l_min[0, 0]

    _attend(q_ref, k_ref, v_ref, qpad_scr, m_scr, l_scr, acc_scr, n_items=2 * n_heads,
            k_tile=lambda i: i // 2, k_half=lambda i: i % 2, v_index=lambda i: i // 2,
            bias_tile=bias_tile, write_out=write_out, far=(near_lo, near_hi, side_bias, m_far, l_far, acc_far))


def _diff_attention(q_t, k, v_t, bias_tiles, table, lq1, lk1, lq2, lk2, subln_gain, *, lambda_init):
    b, _, l = q_t.shape
    tq = Q_BLOCK
    nh = DIFF_HEADS_PER_STEP
    nc = l // KV_CHUNK
    n_tiles = bias_tiles.shape[1]
    rows = nh * 2 * HEAD_DIM
    kern = functools.partial(_diff_attn_kernel, lambda_init=lambda_init, n_heads=nh)
    vec = pl.BlockSpec((1, HEAD_DIM), lambda hp, bi, qi, *_: (0, 0))
    grid_spec = pltpu.PrefetchScalarGridSpec(
        num_scalar_prefetch=1,
        grid=(B_HEADS // nh, b, l // tq),
        in_specs=[
            pl.BlockSpec((1, rows, tq), lambda hp, bi, qi, *_: (bi, hp, qi)),
            pl.BlockSpec((1, l, nh * LANES), lambda hp, bi, qi, *_: (bi, 0, hp)),
            pl.BlockSpec((1, nh, nc, B_V_DIM, KV_CHUNK), lambda hp, bi, qi, *_: (bi, hp, 0, 0, 0)),
            pl.BlockSpec((nh, n_tiles, KV_CHUNK, tq), lambda hp, bi, qi, *_: (hp, 0, 0, 0),
                         pipeline_mode=pl.Buffered(1)),
            vec, vec, vec, vec,
            pl.BlockSpec((B_V_DIM, 1), lambda hp, bi, qi, *_: (0, 0)),
        ],
        out_specs=pl.BlockSpec((1, rows, tq), lambda hp, bi, qi, *_: (bi, hp, qi)),
        scratch_shapes=[
            pltpu.VMEM((2 * nh, 2 * HEAD_DIM, tq), BF16),
            pltpu.VMEM((2 * nh, SUBLANES, tq), F32),
            pltpu.VMEM((2 * nh, SUBLANES, tq), F32),
            pltpu.VMEM((2 * nh, B_V_DIM, tq), F32),
            pltpu.VMEM((2 * nh, SUBLANES, tq), F32),
            pltpu.VMEM((2 * nh, SUBLANES, tq), F32),
            pltpu.VMEM((2 * nh, B_V_DIM, tq), F32),
        ],
    )
    return pl.pallas_call(
        kern, grid_spec=grid_spec,
        out_shape=jax.ShapeDtypeStruct(q_t.shape, BF16),
        compiler_params=_cparams(("parallel", "parallel", "arbitrary")),
        name="diff_attention",
    )(table, q_t, k, v_t, bias_tiles, lq1, lk1, lq2, lk2, subln_gain)


def _bias_tile_kernel(tbl_ref, bkt_ref, o_ref):
    h = pl.program_id(0)
    bkt = bkt_ref[0]
    acc = jnp.zeros(bkt.shape, F32)
    for b in range(NUM_BUCKETS):
        acc = jnp.where(bkt == b, tbl_ref[b, h], acc)
    o_ref[0, 0] = acc


def _bias_tiles(table, buckets):
    n_t, nk, nq = buckets.shape
    n_h = table.shape[1]
    grid_spec = pltpu.PrefetchScalarGridSpec(
        num_scalar_prefetch=1,
        grid=(n_h, n_t),
        in_specs=[pl.BlockSpec((1, nk, nq), lambda h, t, *_: (t, 0, 0))],
        out_specs=pl.BlockSpec((1, 1, nk, nq), lambda h, t, *_: (h, t, 0, 0)),
    )
    return pl.pallas_call(
        _bias_tile_kernel, grid_spec=grid_spec,
        out_shape=jax.ShapeDtypeStruct((n_h, n_t, nk, nq), F32),
        compiler_params=_cparams(("arbitrary", "arbitrary")),
        name="t5_bias_tiles",
    )(table, buckets)


def _post_kernel(a_ref, wo_ref, x_ref, mod_ref, g2_ref, wgu_ref, wd_ref, o_ref, act_scr):
    mod = mod_ref[0, 0]
    y = lax.dot_general(a_ref[0], wo_ref[...], (((0,), (0,)), ((), ())), preferred_element_type=F32)
    x1 = x_ref[0] + mod[2:3] * y
    h = _rms_modulate(x1, g2_ref[0], mod[4:5], mod[3:4]).astype(BF16)
    for c in range(D_FF // FF_CHUNK):
        gu = jnp.dot(h, wgu_ref[0, c], preferred_element_type=F32)
        g, u = gu[:, :FF_CHUNK], gu[:, FF_CHUNK:]
        act_scr[:, c * FF_CHUNK:(c + 1) * FF_CHUNK] = ((g / (1.0 + jnp.exp(-g))) * u).astype(BF16)
    ffn = jnp.dot(act_scr[...], wd_ref[0], preferred_element_type=F32)
    o_ref[0] = x1 + mod[5:6] * ffn


def _post_attention(attn_t, wo, x, mod, layer, seq_off, g2, w_gu, w_d, *, tm):
    b, l, d = x.shape
    once = pl.Buffered(1)
    return pl.pallas_call(
        _post_kernel,
        grid=(b, l // tm),
        in_specs=[
            pl.BlockSpec((1, d, tm), lambda bi, ti: (bi, 0, ti)),
            pl.BlockSpec((d, d), lambda bi, ti: (0, 0), pipeline_mode=once),
            pl.BlockSpec((1, tm, d), lambda bi, ti: (bi, ti, 0)),
            pl.BlockSpec((1, 1, N_MOD, d), lambda bi, ti: (layer, bi + seq_off, 0, 0)),
            pl.BlockSpec((1, 1, d), lambda bi, ti: (layer, 0, 0)),
            pl.BlockSpec((1, D_FF // FF_CHUNK, d, 2 * FF_CHUNK), lambda bi, ti: (layer, 0, 0, 0), pipeline_mode=once),
            pl.BlockSpec((1, D_FF, d), lambda bi, ti: (layer, 0, 0), pipeline_mode=once),
        ],
        out_specs=pl.BlockSpec((1, tm, d), lambda bi, ti: (bi, ti, 0)),
        out_shape=jax.ShapeDtypeStruct(x.shape, F32),
        scratch_shapes=[pltpu.VMEM((tm, D_FF), BF16)],
        compiler_params=_cparams(("parallel", "parallel")),
        name="post_attn_ffn",
    )(attn_t, wo, x, mod, g2, w_gu, w_d)


def _rope_tables(length):
    n_rows = length // GRID_W
    rr, cc = jnp.meshgrid(jnp.arange(n_rows), jnp.arange(GRID_W), indexing="ij")
    rows = rr.reshape(-1).astype(F32)
    cols = cc.reshape(-1).astype(F32)
    n_pairs = HEAD_DIM // 4
    inv_freq = ROPE_THETA ** (-jnp.arange(n_pairs, dtype=F32) / n_pairs)
    ang = jnp.concatenate([rows[:, None] * inv_freq[None], cols[:, None] * inv_freq[None]], axis=-1)
    return jnp.cos(ang).T, jnp.sin(ang).T


def _t5_bucket(rel):
    nb = NUM_BUCKETS // 2
    max_exact = nb // 2
    ret = (rel > 0).astype(jnp.int32) * nb
    n = jnp.abs(rel)
    large = max_exact + (jnp.log(jnp.maximum(n, 1).astype(F32) / max_exact)
                         / math.log(MAX_DISTANCE / max_exact) * (nb - max_exact)).astype(jnp.int32)
    large = jnp.minimum(large, nb - 1)
    return ret + jnp.where(n < max_exact, n, large)


def _bias_buckets():
    assert Q_BLOCK % KV_CHUNK == 0 and KV_CHUNK >= MAX_DISTANCE
    off = jnp.arange(-2, Q_BLOCK // KV_CHUNK + 2, dtype=jnp.int32)[:, None, None]
    kk = jnp.arange(KV_CHUNK, dtype=jnp.int32)[None, :, None]
    qq = jnp.arange(Q_BLOCK, dtype=jnp.int32)[None, None, :]
    return _t5_bucket(off * KV_CHUNK + kk - qq)


def _deinterleave(n_heads):
    base = jnp.concatenate([jnp.arange(0, HEAD_DIM, 2), jnp.arange(1, HEAD_DIM, 2)])
    return (jnp.arange(n_heads)[:, None] * HEAD_DIM + base[None, :]).reshape(-1)


def _col(v):
    return v.astype(F32).reshape(-1, 1)


def _chunk_gate_up(w_gate_up):
    n = D_FF // FF_CHUNK
    gate = w_gate_up[:, :, :D_FF].reshape(DEPTH, D_MODEL, n, FF_CHUNK)
    up = w_gate_up[:, :, D_FF:].reshape(DEPTH, D_MODEL, n, FF_CHUNK)
    return jnp.transpose(jnp.concatenate([gate, up], axis=-1), (0, 2, 1, 3)).astype(BF16)


def _trunk(x, mod, seq_off, p, *, tm):
    _, l, _ = x.shape
    cos_t, sin_t = _rope_tables(l)
    for layer in range(DEPTH):
        if layer % 2 == 0:
            q_t, k, v_t = _pre_attention(
                x, mod, layer, seq_off, p["g1"], p["a_wt"], p["a_gq"], p["a_gk"], cos_t, sin_t,
                n_q=A_HEADS, n_k=A_KV_HEADS, n_v=A_KV_HEADS, v_dim=HEAD_DIM, tm=tm)
            attn_t = _gqa_attention(q_t, k, v_t)
            wo = p["a_wo"]
        else:
            lambda_init = 0.8 - 0.6 * math.exp(-0.3 * layer)
            q_t, k, v_t = _pre_attention(
                x, mod, layer, seq_off, p["g1"], p["b_wt"], p["b_gq"], p["b_gk"], None, None,
                n_q=2 * B_HEADS, n_k=2 * B_HEADS, n_v=B_HEADS, v_dim=B_V_DIM, tm=tm)
            attn_t = _diff_attention(q_t, k, v_t, p["bias_tiles"], p["table"], p["lq1"], p["lk1"], p["lq2"],
                                     p["lk2"], p["subln"], lambda_init=lambda_init)
            wo = p["b_wo"]
        x = _post_attention(attn_t, wo, x, mod, layer, seq_off, p["g2"], p["w_gu"], p["w_d"], tm=tm)
    return x


def kernel(x_prompt, x_sample, c_prompt, c_sample, norm1_gain, norm2_gain, w_ada, b_ada, w_gate_up, w_down, rel_bias, a_w_qkv, a_w_o, a_q_gain, a_k_gain, b_w_qkv, b_w_o, b_q_gain, b_k_gain, b_lq1, b_lk1, b_lq2, b_lk2, b_subln_gain):
    tm = 512
    n_prompt = c_prompt.shape[0]
    n_seq = n_prompt + c_sample.shape[0]
    pad = (-n_seq) % 8
    c_all = jnp.concatenate([c_prompt, c_sample, jnp.zeros((pad, D_MODEL), F32)], axis=0)
    mod = _modulation(c_all, w_ada, b_ada).reshape(DEPTH, n_seq + pad, N_MOD, D_MODEL)

    perm_q = _deinterleave(A_HEADS)
    perm_k = A_HEADS * HEAD_DIM + _deinterleave(A_KV_HEADS)
    v_cols = jnp.arange((A_HEADS + A_KV_HEADS) * HEAD_DIM, (A_HEADS + 2 * A_KV_HEADS) * HEAD_DIM)
    a_perm = jnp.concatenate([perm_q, perm_k, v_cols])
    head_perm = _deinterleave(1)
    table = rel_bias.astype(F32) * LOG2_E
    p = {
        "g1": norm1_gain.reshape(DEPTH, 1, D_MODEL),
        "g2": norm2_gain.reshape(DEPTH, 1, D_MODEL),
        "a_wt": a_w_qkv[0][:, a_perm].T.astype(BF16),
        "a_gq": _col(a_q_gain[0][head_perm]),
        "a_gk": _col(a_k_gain[0][head_perm]),
        "a_wo": a_w_o[0].astype(BF16),
        "b_wt": b_w_qkv[0].T.astype(BF16),
        "b_gq": _col(b_q_gain[0]),
        "b_gk": _col(b_k_gain[0]),
        "b_wo": b_w_o[0].astype(BF16),
        "lq1": b_lq1.astype(F32), "lk1": b_lk1.astype(F32), "lq2": b_lq2.astype(F32), "lk2": b_lk2.astype(F32),
        "subln": _col(b_subln_gain[0]),
        "table": table,
        "bias_tiles": _bias_tiles(table, _bias_buckets()),
        "w_gu": _chunk_gate_up(w_gate_up),
        "w_d": w_down.astype(BF16),
    }
    y_prompt = _trunk(x_prompt, mod, 0, p, tm=tm)
    y_sample = _trunk(x_sample, mod, n_prompt, p, tm=tm)
    return (y_prompt, y_sample)
```

```python
import functools
import math

import jax
import jax.numpy as jnp
from jax import lax
from jax.experimental import pallas as pl
from jax.experimental.pallas import tpu as pltpu

F32 = jnp.float32
BF16 = jnp.bfloat16

D_MODEL = 1024
DEPTH = 2
GRID_W = 64
HEAD_DIM = 64
A_HEADS = 16
A_KV_HEADS = 4
B_HEADS = 8
B_V_DIM = 128
NUM_BUCKETS = 32
MAX_DISTANCE = 128
ROPE_THETA = 10000.0
D_FF = 2816
N_MOD = 6
EPS = 1e-6

LANES = 128
SUBLANES = 8
BF16_ROWS = 16
KV_CHUNK = 256
Q_BLOCK = 512
CHUNKS_PER_BODY = 8
FF_CHUNK = 256
DIFF_HEADS_PER_STEP = 2
QK_LOOKAHEAD = 2
NEG_BIG = -1e30
LOG2_E = math.log2(math.e)
SCORE_GUARD = 30.0 * LOG2_E
DENOM_FLOOR = 1e-30
VMEM_LIMIT = 56 * 1024 * 1024


def _cparams(sem):
    return pltpu.CompilerParams(dimension_semantics=sem, vmem_limit_bytes=VMEM_LIMIT)


def _mod_kernel(c_ref, w_ref, b_ref, o_ref):
    c = c_ref[...]
    c_act = (c / (1.0 + jnp.exp(-c))).astype(BF16)
    w = w_ref[0].astype(BF16)
    o_ref[0] = jnp.dot(c_act, w, preferred_element_type=F32) + b_ref[0]


def _modulation(c_all, w_ada, b_ada):
    s = c_all.shape[0]
    tn = 1024
    n_out = N_MOD * D_MODEL
    return pl.pallas_call(
        _mod_kernel,
        grid=(DEPTH, n_out // tn),
        in_specs=[
            pl.BlockSpec((s, D_MODEL), lambda i, n: (0, 0)),
            pl.BlockSpec((1, D_MODEL, tn), lambda i, n: (i, 0, n)),
            pl.BlockSpec((1, 1, tn), lambda i, n: (i, 0, n)),
        ],
        out_specs=pl.BlockSpec((1, s, tn), lambda i, n: (i, 0, n)),
        out_shape=jax.ShapeDtypeStruct((DEPTH, s, n_out), F32),
        compiler_params=_cparams(("arbitrary", "arbitrary")),
        name="adaln_mod",
    )(c_all, w_ada, b_ada.reshape(DEPTH, 1, n_out))


def _rms_modulate(x, gain, scale, shift):
    ms = jnp.mean(x * x, axis=-1, keepdims=True)
    y = x * lax.rsqrt(ms + EPS)
    return (y * gain) * (1.0 + scale) + shift


def _pre_kernel(x_ref, mod_ref, g1_ref, wt_ref, gq_ref, gk_ref, *rest, n_q, n_k, n_v, v_dim, rope):
    if rope:
        cos_ref, sin_ref, q_ref, k_ref, v_ref = rest
    else:
        q_ref, k_ref, v_ref = rest
    tm = x_ref.shape[1]
    mod = mod_ref[0, 0]
    h = _rms_modulate(x_ref[0], g1_ref[0], mod[1:2], mod[0:1]).astype(BF16)
    qkv_t = lax.dot_general(wt_ref[...], h, (((1,), (1,)), ((), ())), preferred_element_type=F32)

    def head_norm(blk, gain):
        ms = jnp.mean(blk * blk, axis=0, keepdims=True)
        y = blk * lax.rsqrt(ms + EPS) * gain
        if rope:
            half = HEAD_DIM // 2
            x1, x2 = y[:half], y[half:]
            c, s = cos_ref[...], sin_ref[...]
            y = jnp.concatenate([x1 * c - x2 * s, x1 * s + x2 * c], axis=0)
        return y

    gq, gk = gq_ref[...], gk_ref[...]
    for i in range(n_q):
        blk = head_norm(qkv_t[i * HEAD_DIM:(i + 1) * HEAD_DIM], gq)
        q_ref[0, i * HEAD_DIM:(i + 1) * HEAD_DIM, :] = (blk * (HEAD_DIM ** -0.5 * LOG2_E)).astype(BF16)
    k_rows = n_q * HEAD_DIM
    k_t = jnp.concatenate(
        [head_norm(qkv_t[k_rows + i * HEAD_DIM:k_rows + (i + 1) * HEAD_DIM], gk) for i in range(n_k)], axis=0)
    k_ref[0] = k_t.T.astype(BF16)
    v_rows = k_rows + n_k * HEAD_DIM
    for g in range(n_v):
        v_t = qkv_t[v_rows + g * v_dim:v_rows + (g + 1) * v_dim].astype(BF16)
        for c in range(tm // KV_CHUNK):
            v_ref[0, g, c] = v_t[:, c * KV_CHUNK:(c + 1) * KV_CHUNK]


def _pre_attention(x, mod, layer, seq_off, g1, wt, gq, gk, cos_t, sin_t, *, n_q, n_k, n_v, v_dim, tm):
    b, l, d = x.shape
    rope = cos_t is not None
    n_rows = wt.shape[0]
    nc = l // KV_CHUNK
    in_specs = [
        pl.BlockSpec((1, tm, d), lambda bi, ti: (bi, ti, 0)),
        pl.BlockSpec((1, 1, N_MOD, d), lambda bi, ti: (layer, bi + seq_off, 0, 0)),
        pl.BlockSpec((1, 1, d), lambda bi, ti: (layer, 0, 0)),
        pl.BlockSpec((n_rows, d), lambda bi, ti: (0, 0)),
        pl.BlockSpec((HEAD_DIM, 1), lambda bi, ti: (0, 0)),
        pl.BlockSpec((HEAD_DIM, 1), lambda bi, ti: (0, 0)),
    ]
    args = [x, mod, g1, wt, gq, gk]
    if rope:
        in_specs += [pl.BlockSpec((HEAD_DIM // 2, tm), lambda bi, ti: (0, ti))] * 2
        args += [cos_t, sin_t]
    out_shape = (
        jax.ShapeDtypeStruct((b, n_q * HEAD_DIM, l), BF16),
        jax.ShapeDtypeStruct((b, l, n_k * HEAD_DIM), BF16),
        jax.ShapeDtypeStruct((b, n_v, nc, v_dim, KV_CHUNK), BF16),
    )
    out_specs = (
        pl.BlockSpec((1, n_q * HEAD_DIM, tm), lambda bi, ti: (bi, 0, ti)),
        pl.BlockSpec((1, tm, n_k * HEAD_DIM), lambda bi, ti: (bi, ti, 0)),
        pl.BlockSpec((1, n_v, tm // KV_CHUNK, v_dim, KV_CHUNK), lambda bi, ti: (bi, 0, ti, 0, 0)),
    )
    kern = functools.partial(_pre_kernel, n_q=n_q, n_k=n_k, n_v=n_v, v_dim=v_dim, rope=rope)
    return pl.pallas_call(
        kern, grid=(b, l // tm), in_specs=in_specs, out_specs=out_specs, out_shape=out_shape,
        compiler_params=_cparams(("parallel", "parallel")),
        name="pre_attn_rope" if rope else "pre_attn_diff",
    )(*args)


def _attend(q_ref, k_ref, v_ref, qpad_scr, m_scr, l_scr, acc_scr, *, n_items, k_tile, k_half, v_index,
            bias_tile, write_out, far=None):
    nc = v_ref.shape[2]
    tq = q_ref.shape[2]
    assert nc % CHUNKS_PER_BODY == 0
    n_trips = nc // CHUNKS_PER_BODY
    zeros = jnp.zeros((HEAD_DIM, tq), BF16)
    for i in range(n_items):
        q = q_ref[0, i * HEAD_DIM:(i + 1) * HEAD_DIM, :]
        qpad_scr[i] = jnp.concatenate([q, zeros] if k_half(i) == 0 else [zeros, q], axis=0)

    def scores(j, i, biased=True):
        t = k_tile(i)
        kc = k_ref[0, pl.ds(pl.multiple_of(j * KV_CHUNK, KV_CHUNK), KV_CHUNK), t * LANES:(t + 1) * LANES]
        s = jnp.dot(kc, qpad_scr[i], preferred_element_type=F32)
        b = bias_tile(i, j) if biased else None
        return s if b is None else s + b

    def reset(m_ref, l_ref, acc_ref):
        m_ref[...] = jnp.full(m_ref.shape, NEG_BIG, F32)
        l_ref[...] = jnp.zeros(l_ref.shape, F32)
        acc_ref[...] = jnp.zeros(acc_ref.shape, F32)

    def fast_body(biased, m_ref, l_ref, acc_ref):
        def body(t, carry):
            seq = [(CHUNKS_PER_BODY * t + u, i) for i in range(n_items) for u in range(CHUNKS_PER_BODY)]
            pending = [scores(*seq[n], biased) for n in range(QK_LOOKAHEAD)]
            for n, (j, i) in enumerate(seq):
                s = pending[n]
                if n + QK_LOOKAHEAD < len(seq):
                    pending.append(scores(*seq[n + QK_LOOKAHEAD], biased))
                p = jnp.exp2(s)
                m_tile = jnp.max(s.reshape(-1, SUBLANES, tq), axis=0)
                l_tile = jnp.sum(p.reshape(-1, SUBLANES, tq), axis=0)
                pv_tile = jnp.dot(v_ref[0, v_index(i), j], p.astype(BF16), preferred_element_type=F32)
                first = n % CHUNKS_PER_BODY == 0
                m_part = m_tile if first else jnp.maximum(m_part, m_tile)
                l_part = l_tile if first else l_part + l_tile
                pv = pv_tile if first else pv + pv_tile
                if n % CHUNKS_PER_BODY == CHUNKS_PER_BODY - 1:
                    m_ref[i] = jnp.maximum(m_ref[i], m_part)
                    l_ref[i] += l_part
                    acc_ref[i] += pv
            return carry
        return body

    reset(m_scr, l_scr, acc_scr)
    if far is None:
        lax.fori_loop(0, n_trips, fast_body(True, m_scr, l_scr, acc_scr), 0)
    else:
        near_lo, near_hi, side_bias, m_far, l_far, acc_far = far
        reset(m_far, l_far, acc_far)
        lax.fori_loop(0, near_lo, fast_body(False, m_scr, l_scr, acc_scr), 0)
        lax.fori_loop(near_hi, n_trips, fast_body(False, m_far, l_far, acc_far), 0)
        for i in range(n_items):
            c_left, c_right = side_bias(i)
            e_left = jnp.exp2(jnp.full((1, 1), c_left, F32))
            e_right = jnp.exp2(jnp.full((1, 1), c_right, F32))
            acc_scr[i] = e_left * acc_scr[i] + e_right * acc_far[i]
            l_scr[i] = e_left * l_scr[i] + e_right * l_far[i]
            m_scr[i] = jnp.maximum(m_scr[i] + c_left, m_far[i] + c_right)
        lax.fori_loop(near_lo, near_hi, fast_body(True, m_scr, l_scr, acc_scr), 0)
    l_min = write_out(acc_scr, l_scr)
    in_range = jnp.logical_and(jnp.max(m_scr[...]) <= SCORE_GUARD, l_min >= DENOM_FLOOR)

    @pl.when(jnp.logical_not(in_range))
    def _():
        reset(m_scr, l_scr, acc_scr)

        def safe_chunk(j, carry):
            tiles = [scores(j, i) for i in range(n_items)]
            for i in range(n_items):
                s = tiles[i]
                m_old = m_scr[i, 0:1]
                m_new = jnp.maximum(m_old, jnp.max(s, axis=0, keepdims=True))
                alpha = jnp.exp2(m_old - m_new)
                p = jnp.exp2(s - m_new)
                pv = jnp.dot(v_ref[0, v_index(i), j], p.astype(BF16), preferred_element_type=F32)
                acc_scr[i] = alpha * acc_scr[i] + pv
                l_scr[i, 0:1] = alpha * l_scr[i, 0:1] + jnp.sum(p, axis=0, keepdims=True)
                m_scr[i, 0:1] = m_new
            return carry

        lax.fori_loop(0, nc, safe_chunk, 0)
        write_out(acc_scr, l_scr)


def _gqa_attn_kernel(q_ref, k_ref, v_ref, o_ref, qpad_scr, m_scr, l_scr, acc_scr, *, n_items, group):
    def write_out(acc_ref, l_ref):
        l_min = jnp.full((1, 1), jnp.inf, F32)
        for i in range(n_items):
            l = jnp.sum(l_ref[i], axis=0, keepdims=True)
            l_min = jnp.minimum(l_min, jnp.min(l, axis=1, keepdims=True))
            o_ref[0, i * HEAD_DIM:(i + 1) * HEAD_DIM, :] = (acc_ref[i] / l).astype(BF16)
        return l_min[0, 0]

    _attend(q_ref, k_ref, v_ref, qpad_scr, m_scr, l_scr, acc_scr, n_items=n_items,
            k_tile=lambda i: 0, k_half=lambda i: i // group, v_index=lambda i: i // group,
            bias_tile=lambda i, j: None, write_out=write_out)


def _gqa_attention(q_t, k, v_t):
    b, _, l = q_t.shape
    tq = Q_BLOCK
    group = A_HEADS // A_KV_HEADS
    n_items = 2 * group
    rows = n_items * HEAD_DIM
    nc = l // KV_CHUNK
    kern = functools.partial(_gqa_attn_kernel, n_items=n_items, group=group)
    return pl.pallas_call(
        kern,
        grid=(b, A_KV_HEADS // 2, l // tq),
        in_specs=[
            pl.BlockSpec((1, rows, tq), lambda bi, p, qi: (bi, p, qi)),
            pl.BlockSpec((1, l, LANES), lambda bi, p, qi: (bi, 0, p)),
            pl.BlockSpec((1, 2, nc, HEAD_DIM, KV_CHUNK), lambda bi, p, qi: (bi, p, 0, 0, 0)),
        ],
        out_specs=pl.BlockSpec((1, rows, tq), lambda bi, p, qi: (bi, p, qi)),
        out_shape=jax.ShapeDtypeStruct(q_t.shape, BF16),
        scratch_shapes=[
            pltpu.VMEM((n_items, 2 * HEAD_DIM, tq), BF16),
            pltpu.VMEM((n_items, SUBLANES, tq), F32),
            pltpu.VMEM((n_items, SUBLANES, tq), F32),
            pltpu.VMEM((n_items, HEAD_DIM, tq), F32),
        ],
        compiler_params=_cparams(("parallel", "parallel", "arbitrary")),
        name="gqa_attention",
    )(q_t, k, v_t)


def _diff_attn_kernel(tbl_ref, q_ref, k_ref, v_ref, bias_ref, lq1_ref, lk1_ref, lq2_ref, lk2_ref, sg_ref,
                      o_ref, qpad_scr, m_scr, l_scr, acc_scr, m_far, l_far, acc_far, *, lambda_init, n_heads):
    hp = pl.program_id(0)
    qi = pl.program_id(2)
    n_tiles = bias_ref.shape[1]
    q_per_k = Q_BLOCK // KV_CHUNK
    n_trips = v_ref.shape[2] // CHUNKS_PER_BODY

    def bias_tile(i, j):
        return bias_ref[i // 2, jnp.clip(j - qi * q_per_k + 2, 0, n_tiles - 1)]

    def side_bias(i):
        h = hp * n_heads + i // 2
        return tbl_ref[NUM_BUCKETS // 2 - 1, h], tbl_ref[NUM_BUCKETS - 1, h]

    near_lo = jnp.maximum(qi * q_per_k - 1, 0) // CHUNKS_PER_BODY
    near_hi = jnp.minimum((qi * q_per_k + q_per_k) // CHUNKS_PER_BODY + 1, n_trips)

    lam = (jnp.exp(jnp.sum(lq1_ref[...] * lk1_ref[...], axis=1, keepdims=True))
           - jnp.exp(jnp.sum(lq2_ref[...] * lk2_ref[...], axis=1, keepdims=True)) + lambda_init)

    def write_out(acc_ref, l_ref):
        l_min = jnp.full((1, 1), jnp.inf, F32)
        for hh in range(n_heads):
            l0 = jnp.sum(l_ref[2 * hh], axis=0, keepdims=True)
            l1 = jnp.sum(l_ref[2 * hh + 1], axis=0, keepdims=True)
            l_min = jnp.minimum(l_min, jnp.min(jnp.minimum(l0, l1), axis=1, keepdims=True))
            o = acc_ref[2 * hh] / l0 - lam * (acc_ref[2 * hh + 1] / l1)
            ms = jnp.mean(o * o, axis=0, keepdims=True)
            y = (o * lax.rsqrt(ms + EPS)) * sg_ref[...]
            o_ref[0, hh * B_V_DIM:(hh + 1) * B_V_DIM, :] = (y * (1.0 - lambda_init)).astype(BF16)
        return l_min[0, 0]

    _attend(q_ref, k_ref, v_ref, qpad_scr, m_scr, l_scr, acc_scr, n_items=2 * n_heads,
            k_tile=lambda i: i // 2, k_half=lambda i: i % 2, v_index=lambda i: i // 2,
            bias_tile=bias_tile, write_out=write_out, far=(near_lo, near_hi, side_bias, m_far, l_far, acc_far))


def _diff_attention(q_t, k, v_t, bias_tiles, table, lq1, lk1, lq2, lk2, subln_gain, *, lambda_init):
    b, _, l = q_t.shape
    tq = Q_BLOCK
    nh = DIFF_HEADS_PER_STEP
    nc = l // KV_CHUNK
    n_tiles = bias_tiles.shape[1]
    rows = nh * 2 * HEAD_DIM
    kern = functools.partial(_diff_attn_kernel, lambda_init=lambda_init, n_heads=nh)
    vec = pl.BlockSpec((1, HEAD_DIM), lambda hp, bi, qi, *_: (0, 0))
    grid_spec = pltpu.PrefetchScalarGridSpec(
        num_scalar_prefetch=1,
        grid=(B_HEADS // nh, b, l // tq),
        in_specs=[
            pl.BlockSpec((1, rows, tq), lambda hp, bi, qi, *_: (bi, hp, qi)),
            pl.BlockSpec((1, l, nh * LANES), lambda hp, bi, qi, *_: (bi, 0, hp)),
            pl.BlockSpec((1, nh, nc, B_V_DIM, KV_CHUNK), lambda hp, bi, qi, *_: (bi, hp, 0, 0, 0)),
            pl.BlockSpec((nh, n_tiles, KV_CHUNK, tq), lambda hp, bi, qi, *_: (hp, 0, 0, 0),
                         pipeline_mode=pl.Buffered(1)),
            vec, vec, vec, vec,
            pl.BlockSpec((B_V_DIM, 1), lambda hp, bi, qi, *_: (0, 0)),
        ],
        out_specs=pl.BlockSpec((1, rows, tq), lambda hp, bi, qi, *_: (bi, hp, qi)),
        scratch_shapes=[
            pltpu.VMEM((2 * nh, 2 * HEAD_DIM, tq), BF16),
            pltpu.VMEM((2 * nh, SUBLANES, tq), F32),
            pltpu.VMEM((2 * nh, SUBLANES, tq), F32),
            pltpu.VMEM((2 * nh, B_V_DIM, tq), F32),
            pltpu.VMEM((2 * nh, SUBLANES, tq), F32),
            pltpu.VMEM((2 * nh, SUBLANES, tq), F32),
            pltpu.VMEM((2 * nh, B_V_DIM, tq), F32),
        ],
    )
    return pl.pallas_call(
        kern, grid_spec=grid_spec,
        out_shape=jax.ShapeDtypeStruct(q_t.shape, BF16),
        compiler_params=_cparams(("parallel", "parallel", "arbitrary")),
        name="diff_attention",
    )(table, q_t, k, v_t, bias_tiles, lq1, lk1, lq2, lk2, subln_gain)


def _bias_tile_kernel(tbl_ref, bkt_ref, o_ref):
    h = pl.program_id(0)
    bkt = bkt_ref[0]
    acc = jnp.zeros(bkt.shape, F32)
    for b in range(NUM_BUCKETS):
        acc = jnp.where(bkt == b, tbl_ref[b, h], acc)
    o_ref[0, 0] = acc


def _bias_tiles(table, buckets):
    n_t, nk, nq = buckets.shape
    n_h = table.shape[1]
    grid_spec = pltpu.PrefetchScalarGridSpec(
        num_scalar_prefetch=1,
        grid=(n_h, n_t),
        in_specs=[pl.BlockSpec((1, nk, nq), lambda h, t, *_: (t, 0, 0))],
        out_specs=pl.BlockSpec((1, 1, nk, nq), lambda h, t, *_: (h, t, 0, 0)),
    )
    return pl.pallas_call(
        _bias_tile_kernel, grid_spec=grid_spec,
        out_shape=jax.ShapeDtypeStruct((n_h, n_t, nk, nq), F32),
        compiler_params=_cparams(("arbitrary", "arbitrary")),
        name="t5_bias_tiles",
    )(table, buckets)


def _post_kernel(a_ref, wo_ref, x_ref, mod_ref, g2_ref, wgu_ref, wd_ref, o_ref, act_scr):
    mod = mod_ref[0, 0]
    y = lax.dot_general(a_ref[0], wo_ref[...], (((0,), (0,)), ((), ())), preferred_element_type=F32)
    x1 = x_ref[0] + mod[2:3] * y
    h = _rms_modulate(x1, g2_ref[0], mod[4:5], mod[3:4]).astype(BF16)
    for c in range(D_FF // FF_CHUNK):
        gu = jnp.dot(h, wgu_ref[0, c], preferred_element_type=F32)
        g, u = gu[:, :FF_CHUNK], gu[:, FF_CHUNK:]
        act_scr[:, c * FF_CHUNK:(c + 1) * FF_CHUNK] = ((g / (1.0 + jnp.exp(-g))) * u).astype(BF16)
    ffn = jnp.dot(act_scr[...], wd_ref[0], preferred_element_type=F32)
    o_ref[0] = x1 + mod[5:6] * ffn


def _post_attention(attn_t, wo, x, mod, layer, seq_off, g2, w_gu, w_d, *, tm):
    b, l, d = x.shape
    once = pl.Buffered(1)
    return pl.pallas_call(
        _post_kernel,
        grid=(b, l // tm),
        in_specs=[
            pl.BlockSpec((1, d, tm), lambda bi, ti: (bi, 0, ti)),
            pl.BlockSpec((d, d), lambda bi, ti: (0, 0), pipeline_mode=once),
            pl.BlockSpec((1, tm, d), lambda bi, ti: (bi, ti, 0)),
            pl.BlockSpec((1, 1, N_MOD, d), lambda bi, ti: (layer, bi + seq_off, 0, 0)),
            pl.BlockSpec((1, 1, d), lambda bi, ti: (layer, 0, 0)),
            pl.BlockSpec((1, D_FF // FF_CHUNK, d, 2 * FF_CHUNK), lambda bi, ti: (layer, 0, 0, 0), pipeline_mode=once),
            pl.BlockSpec((1, D_FF, d), lambda bi, ti: (layer, 0, 0), pipeline_mode=once),
        ],
        out_specs=pl.BlockSpec((1, tm, d), lambda bi, ti: (bi, ti, 0)),
        out_shape=jax.ShapeDtypeStruct(x.shape, F32),
        scratch_shapes=[pltpu.VMEM((tm, D_FF), BF16)],
        compiler_params=_cparams(("parallel", "parallel")),
        name="post_attn_ffn",
    )(attn_t, wo, x, mod, g2, w_gu, w_d)


def _rope_tables(length):
    n_rows = length // GRID_W
    rr, cc = jnp.meshgrid(jnp.arange(n_rows), jnp.arange(GRID_W), indexing="ij")
    rows = rr.reshape(-1).astype(F32)
    cols = cc.reshape(-1).astype(F32)
    n_pairs = HEAD_DIM // 4
    inv_freq = ROPE_THETA ** (-jnp.arange(n_pairs, dtype=F32) / n_pairs)
    ang = jnp.concatenate([rows[:, None] * inv_freq[None], cols[:, None] * inv_freq[None]], axis=-1)
    return jnp.cos(ang).T, jnp.sin(ang).T


def _t5_bucket(rel):
    nb = NUM_BUCKETS // 2
    max_exact = nb // 2
    ret = (rel > 0).astype(jnp.int32) * nb
    n = jnp.abs(rel)
    large = max_exact + (jnp.log(jnp.maximum(n, 1).astype(F32) / max_exact)
                         / math.log(MAX_DISTANCE / max_exact) * (nb - max_exact)).astype(jnp.int32)
    large = jnp.minimum(large, nb - 1)
    return ret + jnp.where(n < max_exact, n, large)


def _bias_buckets():
    assert Q_BLOCK % KV_CHUNK == 0 and KV_CHUNK >= MAX_DISTANCE
    off = jnp.arange(-2, Q_BLOCK // KV_CHUNK + 2, dtype=jnp.int32)[:, None, None]
    kk = jnp.arange(KV_CHUNK, dtype=jnp.int32)[None, :, None]
    qq = jnp.arange(Q_BLOCK, dtype=jnp.int32)[None, None, :]
    return _t5_bucket(off * KV_CHUNK + kk - qq)


def _deinterleave(n_heads):
    base = jnp.concatenate([jnp.arange(0, HEAD_DIM, 2), jnp.arange(1, HEAD_DIM, 2)])
    return (jnp.arange(n_heads)[:, None] * HEAD_DIM + base[None, :]).reshape(-1)


def _col(v):
    return v.astype(F32).reshape(-1, 1)


def _chunk_gate_up(w_gate_up):
    n = D_FF // FF_CHUNK
    gate = w_gate_up[:, :, :D_FF].reshape(DEPTH, D_MODEL, n, FF_CHUNK)
    up = w_gate_up[:, :, D_FF:].reshape(DEPTH, D_MODEL, n, FF_CHUNK)
    return jnp.transpose(jnp.concatenate([gate, up], axis=-1), (0, 2, 1, 3)).astype(BF16)


def _trunk(x, mod, seq_off, p, *, tm):
    _, l, _ = x.shape
    cos_t, sin_t = _rope_tables(l)
    for layer in range(DEPTH):
        if layer % 2 == 0:
            q_t, k, v_t = _pre_attention(
                x, mod, layer, seq_off, p["g1"], p["a_wt"], p["a_gq"], p["a_gk"], cos_t, sin_t,
                n_q=A_HEADS, n_k=A_KV_HEADS, n_v=A_KV_HEADS, v_dim=HEAD_DIM, tm=tm)
            attn_t = _gqa_attention(q_t, k, v_t)
            wo = p["a_wo"]
        else:
            lambda_init = 0.8 - 0.6 * math.exp(-0.3 * layer)
            q_t, k, v_t = _pre_attention(
                x, mod, layer, seq_off, p["g1"], p["b_wt"], p["b_gq"], p["b_gk"], None, None,
                n_q=2 * B_HEADS, n_k=2 * B_HEADS, n_v=B_HEADS, v_dim=B_V_DIM, tm=tm)
            attn_t = _diff_attention(q_t, k, v_t, p["bias_tiles"], p["table"], p["lq1"], p["lk1"], p["lq2"],
                                     p["lk2"], p["subln"], lambda_init=lambda_init)
            wo = p["b_wo"]
        x = _post_attention(attn_t, wo, x, mod, layer, seq_off, p["g2"], p["w_gu"], p["w_d"], tm=tm)
    return x


def kernel(x_prompt, x_sample, c_prompt, c_sample, norm1_gain, norm2_gain, w_ada, b_ada, w_gate_up, w_down, rel_bias, a_w_qkv, a_w_o, a_q_gain, a_k_gain, b_w_qkv, b_w_o, b_q_gain, b_k_gain, b_lq1, b_lk1, b_lq2, b_lk2, b_subln_gain):
    tm = 512
    n_prompt = c_prompt.shape[0]
    n_seq = n_prompt + c_sample.shape[0]
    pad = (-n_seq) % 8
    c_all = jnp.concatenate([c_prompt, c_sample, jnp.zeros((pad, D_MODEL), F32)], axis=0)
    mod = _modulation(c_all, w_ada, b_ada).reshape(DEPTH, n_seq + pad, N_MOD, D_MODEL)

    perm_q = _deinterleave(A_HEADS)
    perm_k = A_HEADS * HEAD_DIM + _deinterleave(A_KV_HEADS)
    v_cols = jnp.arange((A_HEADS + A_KV_HEADS) * HEAD_DIM, (A_HEADS + 2 * A_KV_HEADS) * HEAD_DIM)
    a_perm = jnp.concatenate([perm_q, perm_k, v_cols])
    head_perm = _deinterleave(1)
    table = rel_bias.astype(F32) * LOG2_E
    p = {
        "g1": norm1_gain.reshape(DEPTH, 1, D_MODEL),
        "g2": norm2_gain.reshape(DEPTH, 1, D_MODEL),
        "a_wt": a_w_qkv[0][:, a_perm].T.astype(BF16),
        "a_gq": _col(a_q_gain[0][head_perm]),
        "a_gk": _col(a_k_gain[0][head_perm]),
        "a_wo": a_w_o[0].astype(BF16),
        "b_wt": b_w_qkv[0].T.astype(BF16),
        "b_gq": _col(b_q_gain[0]),
        "b_gk": _col(b_k_gain[0]),
        "b_wo": b_w_o[0].astype(BF16),
        "lq1": b_lq1.astype(F32), "lk1": b_lk1.astype(F32), "lq2": b_lq2.astype(F32), "lk2": b_lk2.astype(F32),
        "subln": _col(b_subln_gain[0]),
        "table": table,
        "bias_tiles": _bias_tiles(table, _bias_buckets()),
        "w_gu": _chunk_gate_up(w_gate_up),
        "w_d": w_down.astype(BF16),
    }
    y_prompt = _trunk(x_prompt, mod, 0, p, tm=tm)
    y_sample = _trunk(x_sample, mod, n_prompt, p, tm=tm)
    return (y_prompt, y_sample)
```

```python
import functools
import math

import jax
import jax.numpy as jnp
from jax import lax
from jax.experimental import pallas as pl
from jax.experimental.pallas import tpu as pltpu

F32 = jnp.float32
BF16 = jnp.bfloat16

D_MODEL = 1024
DEPTH = 2
GRID_W = 64
HEAD_DIM = 64
A_HEADS = 16
A_KV_HEADS = 4
B_HEADS = 8
B_V_DIM = 128
NUM_BUCKETS = 32
MAX_DISTANCE = 128
ROPE_THETA = 10000.0
D_FF = 2816
N_MOD = 6
EPS = 1e-6

LANES = 128
SUBLANES = 8
BF16_ROWS = 16
KV_CHUNK = 256
Q_BLOCK = 512
CHUNKS_PER_BODY = 8
FF_CHUNK = 256
PRE_ROW_GROUP = 512
DIFF_HEADS_PER_STEP = 2
QK_LOOKAHEAD = 2
NEG_BIG = -1e30
LOG2_E = math.log2(math.e)
SCORE_GUARD = 30.0 * LOG2_E
DENOM_FLOOR = 1e-30
VMEM_LIMIT = 56 * 1024 * 1024


def _cparams(sem):
    return pltpu.CompilerParams(dimension_semantics=sem, vmem_limit_bytes=VMEM_LIMIT)


def _mod_kernel(c_ref, w_ref, b_ref, o_ref):
    c = c_ref[...]
    c_act = (c / (1.0 + jnp.exp(-c))).astype(BF16)
    w = w_ref[0].astype(BF16)
    o_ref[0] = jnp.dot(c_act, w, preferred_element_type=F32) + b_ref[0]


def _modulation(c_all, w_ada, b_ada):
    s = c_all.shape[0]
    tn = 1024
    n_out = N_MOD * D_MODEL
    return pl.pallas_call(
        _mod_kernel,
        grid=(DEPTH, n_out // tn),
        in_specs=[
            pl.BlockSpec((s, D_MODEL), lambda i, n: (0, 0)),
            pl.BlockSpec((1, D_MODEL, tn), lambda i, n: (i, 0, n)),
            pl.BlockSpec((1, 1, tn), lambda i, n: (i, 0, n)),
        ],
        out_specs=pl.BlockSpec((1, s, tn), lambda i, n: (i, 0, n)),
        out_shape=jax.ShapeDtypeStruct((DEPTH, s, n_out), F32),
        compiler_params=_cparams(("arbitrary", "arbitrary")),
        name="adaln_mod",
    )(c_all, w_ada, b_ada.reshape(DEPTH, 1, n_out))


def _rms_modulate(x, gain, scale, shift):
    ms = jnp.mean(x * x, axis=-1, keepdims=True)
    y = x * lax.rsqrt(ms + EPS)
    return (y * gain) * (1.0 + scale) + shift


def _pre_kernel(x_ref, mod_ref, g1_ref, wt_ref, gq_ref, gk_ref, *rest, n_q, n_k, n_v, v_dim, rope):
    if rope:
        cos_ref, sin_ref, q_ref, k_ref, v_ref = rest
    else:
        q_ref, k_ref, v_ref = rest
    tm = x_ref.shape[1]
    mod = mod_ref[0, 0]
    h = _rms_modulate(x_ref[0], g1_ref[0], mod[1:2], mod[0:1]).astype(BF16)
    groups = [lax.dot_general(wt_ref[r:r + PRE_ROW_GROUP], h, (((1,), (1,)), ((), ())), preferred_element_type=F32)
              for r in range(0, wt_ref.shape[0], PRE_ROW_GROUP)]

    def rows(start, size):
        g, off = divmod(start, PRE_ROW_GROUP)
        assert off + size <= PRE_ROW_GROUP
        return groups[g][off:off + size]

    def head_norm(blk, gain):
        ms = jnp.mean(blk * blk, axis=0, keepdims=True)
        y = blk * lax.rsqrt(ms + EPS) * gain
        if rope:
            half = HEAD_DIM // 2
            x1, x2 = y[:half], y[half:]
            c, s = cos_ref[...], sin_ref[...]
            y = jnp.concatenate([x1 * c - x2 * s, x1 * s + x2 * c], axis=0)
        return y

    gq, gk = gq_ref[...], gk_ref[...]
    for i in range(n_q):
        blk = head_norm(rows(i * HEAD_DIM, HEAD_DIM), gq)
        q_ref[0, i * HEAD_DIM:(i + 1) * HEAD_DIM, :] = (blk * (HEAD_DIM ** -0.5 * LOG2_E)).astype(BF16)
    k_rows = n_q * HEAD_DIM
    k_t = jnp.concatenate(
        [head_norm(rows(k_rows + i * HEAD_DIM, HEAD_DIM), gk) for i in range(n_k)], axis=0)
    k_ref[0] = k_t.T.astype(BF16)
    v_rows = k_rows + n_k * HEAD_DIM
    for g in range(n_v):
        v_t = rows(v_rows + g * v_dim, v_dim).astype(BF16)
        for c in range(tm // KV_CHUNK):
            v_ref[0, g, c] = v_t[:, c * KV_CHUNK:(c + 1) * KV_CHUNK]


def _pre_attention(x, mod, layer, seq_off, g1, wt, gq, gk, cos_t, sin_t, *, n_q, n_k, n_v, v_dim, tm):
    b, l, d = x.shape
    rope = cos_t is not None
    n_rows = wt.shape[0]
    nc = l // KV_CHUNK
    in_specs = [
        pl.BlockSpec((1, tm, d), lambda bi, ti: (bi, ti, 0)),
        pl.BlockSpec((1, 1, N_MOD, d), lambda bi, ti: (layer, bi + seq_off, 0, 0)),
        pl.BlockSpec((1, 1, d), lambda bi, ti: (layer, 0, 0)),
        pl.BlockSpec((n_rows, d), lambda bi, ti: (0, 0)),
        pl.BlockSpec((HEAD_DIM, 1), lambda bi, ti: (0, 0)),
        pl.BlockSpec((HEAD_DIM, 1), lambda bi, ti: (0, 0)),
    ]
    args = [x, mod, g1, wt, gq, gk]
    if rope:
        in_specs += [pl.BlockSpec((HEAD_DIM // 2, tm), lambda bi, ti: (0, ti))] * 2
        args += [cos_t, sin_t]
    out_shape = (
        jax.ShapeDtypeStruct((b, n_q * HEAD_DIM, l), BF16),
        jax.ShapeDtypeStruct((b, l, n_k * HEAD_DIM), BF16),
        jax.ShapeDtypeStruct((b, n_v, nc, v_dim, KV_CHUNK), BF16),
    )
    out_specs = (
        pl.BlockSpec((1, n_q * HEAD_DIM, tm), lambda bi, ti: (bi, 0, ti)),
        pl.BlockSpec((1, tm, n_k * HEAD_DIM), lambda bi, ti: (bi, ti, 0)),
        pl.BlockSpec((1, n_v, tm // KV_CHUNK, v_dim, KV_CHUNK), lambda bi, ti: (bi, 0, ti, 0, 0)),
    )
    kern = functools.partial(_pre_kernel, n_q=n_q, n_k=n_k, n_v=n_v, v_dim=v_dim, rope=rope)
    return pl.pallas_call(
        kern, grid=(b, l // tm), in_specs=in_specs, out_specs=out_specs, out_shape=out_shape,
        compiler_params=_cparams(("parallel", "parallel")),
        name="pre_attn_rope" if rope else "pre_attn_diff",
    )(*args)


def _attend(q_ref, k_ref, v_ref, qpad_scr, m_scr, l_scr, acc_scr, *, n_items, k_tile, k_half, v_index,
            bias_tile, write_out, far=None):
    nc = v_ref.shape[2]
    tq = q_ref.shape[2]
    assert nc % CHUNKS_PER_BODY == 0
    n_trips = nc // CHUNKS_PER_BODY
    zeros = jnp.zeros((HEAD_DIM, tq), BF16)
    for i in range(n_items):
        q = q_ref[0, i * HEAD_DIM:(i + 1) * HEAD_DIM, :]
        qpad_scr[i] = jnp.concatenate([q, zeros] if k_half(i) == 0 else [zeros, q], axis=0)

    def scores(j, i, biased=True):
        t = k_tile(i)
        kc = k_ref[0, pl.ds(pl.multiple_of(j * KV_CHUNK, KV_CHUNK), KV_CHUNK), t * LANES:(t + 1) * LANES]
        s = jnp.dot(kc, qpad_scr[i], preferred_element_type=F32)
        b = bias_tile(i, j) if biased else None
        return s if b is None else s + b

    def reset(m_ref, l_ref, acc_ref):
        m_ref[...] = jnp.full(m_ref.shape, NEG_BIG, F32)
        l_ref[...] = jnp.zeros(l_ref.shape, F32)
        acc_ref[...] = jnp.zeros(acc_ref.shape, F32)

    def fast_body(biased, m_ref, l_ref, acc_ref):
        def body(t, carry):
            seq = [(CHUNKS_PER_BODY * t + u, i) for i in range(n_items) for u in range(CHUNKS_PER_BODY)]
            pending = [scores(*seq[n], biased) for n in range(QK_LOOKAHEAD)]
            for n, (j, i) in enumerate(seq):
                s = pending[n]
                if n + QK_LOOKAHEAD < len(seq):
                    pending.append(scores(*seq[n + QK_LOOKAHEAD], biased))
                p = jnp.exp2(s)
                m_tile = jnp.max(s.reshape(-1, SUBLANES, tq), axis=0)
                l_tile = jnp.sum(p.reshape(-1, SUBLANES, tq), axis=0)
                pv_tile = jnp.dot(v_ref[0, v_index(i), j], p.astype(BF16), preferred_element_type=F32)
                first = n % CHUNKS_PER_BODY == 0
                m_part = m_tile if first else jnp.maximum(m_part, m_tile)
                l_part = l_tile if first else l_part + l_tile
                pv = pv_tile if first else pv + pv_tile
                if n % CHUNKS_PER_BODY == CHUNKS_PER_BODY - 1:
                    m_ref[i] = jnp.maximum(m_ref[i], m_part)
                    l_ref[i] += l_part
                    acc_ref[i] += pv
            return carry
        return body

    reset(m_scr, l_scr, acc_scr)
    if far is None:
        lax.fori_loop(0, n_trips, fast_body(True, m_scr, l_scr, acc_scr), 0)
    else:
        near_lo, near_hi, side_bias, m_far, l_far, acc_far = far
        reset(m_far, l_far, acc_far)
        lax.fori_loop(0, near_lo, fast_body(False, m_scr, l_scr, acc_scr), 0)
        lax.fori_loop(near_hi, n_trips, fast_body(False, m_far, l_far, acc_far), 0)
        for i in range(n_items):
            c_left, c_right = side_bias(i)
            e_left = jnp.exp2(jnp.full((1, 1), c_left, F32))
            e_right = jnp.exp2(jnp.full((1, 1), c_right, F32))
            acc_scr[i] = e_left * acc_scr[i] + e_right * acc_far[i]
            l_scr[i] = e_left * l_scr[i] + e_right * l_far[i]
            m_scr[i] = jnp.maximum(m_scr[i] + c_left, m_far[i] + c_right)
        lax.fori_loop(near_lo, near_hi, fast_body(True, m_scr, l_scr, acc_scr), 0)
    l_min = write_out(acc_scr, l_scr)
    in_range = jnp.logical_and(jnp.max(m_scr[...]) <= SCORE_GUARD, l_min >= DENOM_FLOOR)

    @pl.when(jnp.logical_not(in_range))
    def _():
        reset(m_scr, l_scr, acc_scr)

        def safe_chunk(j, carry):
            tiles = [scores(j, i) for i in range(n_items)]
            for i in range(n_items):
                s = tiles[i]
                m_old = m_scr[i, 0:1]
                m_new = jnp.maximum(m_old, jnp.max(s, axis=0, keepdims=True))
                alpha = jnp.exp2(m_old - m_new)
                p = jnp.exp2(s - m_new)
                pv = jnp.dot(v_ref[0, v_index(i), j], p.astype(BF16), preferred_element_type=F32)
                acc_scr[i] = alpha * acc_scr[i] + pv
                l_scr[i, 0:1] = alpha * l_scr[i, 0:1] + jnp.sum(p, axis=0, keepdims=True)
                m_scr[i, 0:1] = m_new
            return carry

        lax.fori_loop(0, nc, safe_chunk, 0)
        write_out(acc_scr, l_scr)


def _gqa_attn_kernel(q_ref, k_ref, v_ref, o_ref, qpad_scr, m_scr, l_scr, acc_scr, *, n_items, group):
    def write_out(acc_ref, l_ref):
        l_min = jnp.full((1, 1), jnp.inf, F32)
        for i in range(n_items):
            l = jnp.sum(l_ref[i], axis=0, keepdims=True)
            l_min = jnp.minimum(l_min, jnp.min(l, axis=1, keepdims=True))
            o_ref[0, i * HEAD_DIM:(i + 1) * HEAD_DIM, :] = (acc_ref[i] / l).astype(BF16)
        return l_min[0, 0]

    _attend(q_ref, k_ref, v_ref, qpad_scr, m_scr, l_scr, acc_scr, n_items=n_items,
            k_tile=lambda i: 0, k_half=lambda i: i // group, v_index=lambda i: i // group,
            bias_tile=lambda i, j: None, write_out=write_out)


def _gqa_attention(q_t, k, v_t):
    b, _, l = q_t.shape
    tq = Q_BLOCK
    group = A_HEADS // A_KV_HEADS
    n_items = 2 * group
    rows = n_items * HEAD_DIM
    nc = l // KV_CHUNK
    kern = functools.partial(_gqa_attn_kernel, n_items=n_items, group=group)
    return pl.pallas_call(
        kern,
        grid=(b, A_KV_HEADS // 2, l // tq),
        in_specs=[
            pl.BlockSpec((1, rows, tq), lambda bi, p, qi: (bi, p, qi)),
            pl.BlockSpec((1, l, LANES), lambda bi, p, qi: (bi, 0, p)),
            pl.BlockSpec((1, 2, nc, HEAD_DIM, KV_CHUNK), lambda bi, p, qi: (bi, p, 0, 0, 0)),
        ],
        out_specs=pl.BlockSpec((1, rows, tq), lambda bi, p, qi: (bi, p, qi)),
        out_shape=jax.ShapeDtypeStruct(q_t.shape, BF16),
        scratch_shapes=[
            pltpu.VMEM((n_items, 2 * HEAD_DIM, tq), BF16),
            pltpu.VMEM((n_items, SUBLANES, tq), F32),
            pltpu.VMEM((n_items, SUBLANES, tq), F32),
            pltpu.VMEM((n_items, HEAD_DIM, tq), F32),
        ],
        compiler_params=_cparams(("parallel", "parallel", "arbitrary")),
        name="gqa_attention",
    )(q_t, k, v_t)


def _diff_attn_kernel(tbl_ref, q_ref, k_ref, v_ref, bias_ref, lq1_ref, lk1_ref, lq2_ref, lk2_ref, sg_ref,
                      o_ref, qpad_scr, m_scr, l_scr, acc_scr, m_far, l_far, acc_far, *, lambda_init, n_heads):
    hp = pl.program_id(0)
    qi = pl.program_id(2)
    n_tiles = bias_ref.shape[1]
    q_per_k = Q_BLOCK // KV_CHUNK
    n_trips = v_ref.shape[2] // CHUNKS_PER_BODY

    def bias_tile(i, j):
        return bias_ref[i // 2, jnp.clip(j - qi * q_per_k + 2, 0, n_tiles - 1)]

    def side_bias(i):
        h = hp * n_heads + i // 2
        return tbl_ref[NUM_BUCKETS // 2 - 1, h], tbl_ref[NUM_BUCKETS - 1, h]

    near_lo = jnp.maximum(qi * q_per_k - 1, 0) // CHUNKS_PER_BODY
    near_hi = jnp.minimum((qi * q_per_k + q_per_k) // CHUNKS_PER_BODY + 1, n_trips)

    lam = (jnp.exp(jnp.sum(lq1_ref[...] * lk1_ref[...], axis=1, keepdims=True))
           - jnp.exp(jnp.sum(lq2_ref[...] * lk2_ref[...], axis=1, keepdims=True)) + lambda_init)

    def write_out(acc_ref, l_ref):
        l_min = jnp.full((1, 1), jnp.inf, F32)
        for hh in range(n_heads):
            l0 = jnp.sum(l_ref[2 * hh], axis=0, keepdims=True)
            l1 = jnp.sum(l_ref[2 * hh + 1], axis=0, keepdims=True)
            l_min = jnp.minimum(l_min, jnp.min(jnp.minimum(l0, l1), axis=1, keepdims=True))
            o = acc_ref[2 * hh] / l0 - lam * (acc_ref[2 * hh + 1] / l1)
            ms = jnp.mean(o * o, axis=0, keepdims=True)
            y = (o * lax.rsqrt(ms + EPS)) * sg_ref[...]
            o_ref[0, hh * B_V_DIM:(hh + 1) * B_V_DIM, :] = (y * (1.0 - lambda_init)).astype(BF16)
        return l_min[0, 0]

    _attend(q_ref, k_ref, v_ref, qpad_scr, m_scr, l_scr, acc_scr, n_items=2 * n_heads,
            k_tile=lambda i: i // 2, k_half=lambda i: i % 2, v_index=lambda i: i // 2,
            bias_tile=bias_tile, write_out=write_out,
            far=None if n_trips == 1 else (near_lo, near_hi, side_bias, m_far, l_far, acc_far))


def _diff_attention(q_t, k, v_t, bias_tiles, table, lq1, lk1, lq2, lk2, subln_gain, *, lambda_init):
    b, _, l = q_t.shape
    tq = Q_BLOCK
    nh = DIFF_HEADS_PER_STEP
    nc = l // KV_CHUNK
    n_tiles = bias_tiles.shape[1]
    rows = nh * 2 * HEAD_DIM
    kern = functools.partial(_diff_attn_kernel, lambda_init=lambda_init, n_heads=nh)
    vec = pl.BlockSpec((1, HEAD_DIM), lambda hp, bi, qi, *_: (0, 0))
    grid_spec = pltpu.PrefetchScalarGridSpec(
        num_scalar_prefetch=1,
        grid=(B_HEADS // nh, b, l // tq),
        in_specs=[
            pl.BlockSpec((1, rows, tq), lambda hp, bi, qi, *_: (bi, hp, qi)),
            pl.BlockSpec((1, l, nh * LANES), lambda hp, bi, qi, *_: (bi, 0, hp)),
            pl.BlockSpec((1, nh, nc, B_V_DIM, KV_CHUNK), lambda hp, bi, qi, *_: (bi, hp, 0, 0, 0)),
            pl.BlockSpec((nh, n_tiles, KV_CHUNK, tq), lambda hp, bi, qi, *_: (hp, 0, 0, 0),
                         pipeline_mode=pl.Buffered(1)),
            vec, vec, vec, vec,
            pl.BlockSpec((B_V_DIM, 1), lambda hp, bi, qi, *_: (0, 0)),
        ],
        out_specs=pl.BlockSpec((1, rows, tq), lambda hp, bi, qi, *_: (bi, hp, qi)),
        scratch_shapes=[
            pltpu.VMEM((2 * nh, 2 * HEAD_DIM, tq), BF16),
            pltpu.VMEM((2 * nh, SUBLANES, tq), F32),
            pltpu.VMEM((2 * nh, SUBLANES, tq), F32),
            pltpu.VMEM((2 * nh, B_V_DIM, tq), F32),
            pltpu.VMEM((2 * nh, SUBLANES, tq), F32),
            pltpu.VMEM((2 * nh, SUBLANES, tq), F32),
            pltpu.VMEM((2 * nh, B_V_DIM, tq), F32),
        ],
    )
    return pl.pallas_call(
        kern, grid_spec=grid_spec,
        out_shape=jax.ShapeDtypeStruct(q_t.shape, BF16),
        compiler_params=_cparams(("parallel", "parallel", "arbitrary")),
        name="diff_attention",
    )(table, q_t, k, v_t, bias_tiles, lq1, lk1, lq2, lk2, subln_gain)


def _bias_tile_kernel(tbl_ref, bkt_ref, o_ref):
    h = pl.program_id(0)
    bkt = bkt_ref[0]
    acc = jnp.zeros(bkt.shape, F32)
    for b in range(NUM_BUCKETS):
        acc = jnp.where(bkt == b, tbl_ref[b, h], acc)
    o_ref[0, 0] = acc


def _bias_tiles(table, buckets):
    n_t, nk, nq = buckets.shape
    n_h = table.shape[1]
    grid_spec = pltpu.PrefetchScalarGridSpec(
        num_scalar_prefetch=1,
        grid=(n_h, n_t),
        in_specs=[pl.BlockSpec((1, nk, nq), lambda h, t, *_: (t, 0, 0))],
        out_specs=pl.BlockSpec((1, 1, nk, nq), lambda h, t, *_: (h, t, 0, 0)),
    )
    return pl.pallas_call(
        _bias_tile_kernel, grid_spec=grid_spec,
        out_shape=jax.ShapeDtypeStruct((n_h, n_t, nk, nq), F32),
        compiler_params=_cparams(("arbitrary", "arbitrary")),
        name="t5_bias_tiles",
    )(table, buckets)


def _post_kernel(a_ref, wo_ref, x_ref, mod_ref, g2_ref, wgu_ref, wd_ref, o_ref, act_scr):
    mod = mod_ref[0, 0]
    y = lax.dot_general(a_ref[0], wo_ref[...], (((0,), (0,)), ((), ())), preferred_element_type=F32)
    x1 = x_ref[0] + mod[2:3] * y
    h = _rms_modulate(x1, g2_ref[0], mod[4:5], mod[3:4]).astype(BF16)
    for c in range(D_FF // FF_CHUNK):
        gu = jnp.dot(h, wgu_ref[0, c], preferred_element_type=F32)
        g, u = gu[:, :FF_CHUNK], gu[:, FF_CHUNK:]
        act_scr[:, c * FF_CHUNK:(c + 1) * FF_CHUNK] = ((g / (1.0 + jnp.exp(-g))) * u).astype(BF16)
    ffn = jnp.dot(act_scr[...], wd_ref[0], preferred_element_type=F32)
    o_ref[0] = x1 + mod[5:6] * ffn


def _post_attention(attn_t, wo, x, mod, layer, seq_off, g2, w_gu, w_d, *, tm):
    b, l, d = x.shape
    once = pl.Buffered(1)
    return pl.pallas_call(
        _post_kernel,
        grid=(b, l // tm),
        in_specs=[
            pl.BlockSpec((1, d, tm), lambda bi, ti: (bi, 0, ti)),
            pl.BlockSpec((d, d), lambda bi, ti: (0, 0), pipeline_mode=once),
            pl.BlockSpec((1, tm, d), lambda bi, ti: (bi, ti, 0)),
            pl.BlockSpec((1, 1, N_MOD, d), lambda bi, ti: (layer, bi + seq_off, 0, 0)),
            pl.BlockSpec((1, 1, d), lambda bi, ti: (layer, 0, 0)),
            pl.BlockSpec((1, D_FF // FF_CHUNK, d, 2 * FF_CHUNK), lambda bi, ti: (layer, 0, 0, 0), pipeline_mode=once),
            pl.BlockSpec((1, D_FF, d), lambda bi, ti: (layer, 0, 0), pipeline_mode=once),
        ],
        out_specs=pl.BlockSpec((1, tm, d), lambda bi, ti: (bi, ti, 0)),
        out_shape=jax.ShapeDtypeStruct(x.shape, F32),
        scratch_shapes=[pltpu.VMEM((tm, D_FF), BF16)],
        compiler_params=_cparams(("parallel", "parallel")),
        name="post_attn_ffn",
    )(attn_t, wo, x, mod, g2, w_gu, w_d)


def _rope_tables(length):
    n_rows = length // GRID_W
    rr, cc = jnp.meshgrid(jnp.arange(n_rows), jnp.arange(GRID_W), indexing="ij")
    rows = rr.reshape(-1).astype(F32)
    cols = cc.reshape(-1).astype(F32)
    n_pairs = HEAD_DIM // 4
    inv_freq = ROPE_THETA ** (-jnp.arange(n_pairs, dtype=F32) / n_pairs)
    ang = jnp.concatenate([rows[:, None] * inv_freq[None], cols[:, None] * inv_freq[None]], axis=-1)
    return jnp.cos(ang).T, jnp.sin(ang).T


def _t5_bucket(rel):
    nb = NUM_BUCKETS // 2
    max_exact = nb // 2
    ret = (rel > 0).astype(jnp.int32) * nb
    n = jnp.abs(rel)
    large = max_exact + (jnp.log(jnp.maximum(n, 1).astype(F32) / max_exact)
                         / math.log(MAX_DISTANCE / max_exact) * (nb - max_exact)).astype(jnp.int32)
    large = jnp.minimum(large, nb - 1)
    return ret + jnp.where(n < max_exact, n, large)


def _bias_buckets():
    assert Q_BLOCK % KV_CHUNK == 0 and KV_CHUNK >= MAX_DISTANCE
    off = jnp.arange(-2, Q_BLOCK // KV_CHUNK + 2, dtype=jnp.int32)[:, None, None]
    kk = jnp.arange(KV_CHUNK, dtype=jnp.int32)[None, :, None]
    qq = jnp.arange(Q_BLOCK, dtype=jnp.int32)[None, None, :]
    return _t5_bucket(off * KV_CHUNK + kk - qq)


def _deinterleave(n_heads):
    base = jnp.concatenate([jnp.arange(0, HEAD_DIM, 2), jnp.arange(1, HEAD_DIM, 2)])
    return (jnp.arange(n_heads)[:, None] * HEAD_DIM + base[None, :]).reshape(-1)


def _col(v):
    return v.astype(F32).reshape(-1, 1)


def _chunk_gate_up(w_gate_up):
    n = D_FF // FF_CHUNK
    gate = w_gate_up[:, :, :D_FF].reshape(DEPTH, D_MODEL, n, FF_CHUNK)
    up = w_gate_up[:, :, D_FF:].reshape(DEPTH, D_MODEL, n, FF_CHUNK)
    return jnp.transpose(jnp.concatenate([gate, up], axis=-1), (0, 2, 1, 3)).astype(BF16)


def _trunk(x, mod, seq_off, p, *, tm):
    _, l, _ = x.shape
    cos_t, sin_t = _rope_tables(l)
    for layer in range(DEPTH):
        if layer % 2 == 0:
            q_t, k, v_t = _pre_attention(
                x, mod, layer, seq_off, p["g1"], p["a_wt"], p["a_gq"], p["a_gk"], cos_t, sin_t,
                n_q=A_HEADS, n_k=A_KV_HEADS, n_v=A_KV_HEADS, v_dim=HEAD_DIM, tm=tm)
            attn_t = _gqa_attention(q_t, k, v_t)
            wo = p["a_wo"]
        else:
            lambda_init = 0.8 - 0.6 * math.exp(-0.3 * layer)
            q_t, k, v_t = _pre_attention(
                x, mod, layer, seq_off, p["g1"], p["b_wt"], p["b_gq"], p["b_gk"], None, None,
                n_q=2 * B_HEADS, n_k=2 * B_HEADS, n_v=B_HEADS, v_dim=B_V_DIM, tm=tm)
            attn_t = _diff_attention(q_t, k, v_t, p["bias_tiles"], p["table"], p["lq1"], p["lk1"], p["lq2"],
                                     p["lk2"], p["subln"], lambda_init=lambda_init)
            wo = p["b_wo"]
        x = _post_attention(attn_t, wo, x, mod, layer, seq_off, p["g2"], p["w_gu"], p["w_d"], tm=tm)
    return x


def kernel(x_prompt, x_sample, c_prompt, c_sample, norm1_gain, norm2_gain, w_ada, b_ada, w_gate_up, w_down, rel_bias, a_w_qkv, a_w_o, a_q_gain, a_k_gain, b_w_qkv, b_w_o, b_q_gain, b_k_gain, b_lq1, b_lk1, b_lq2, b_lk2, b_subln_gain):
    tm = 512
    n_prompt = c_prompt.shape[0]
    n_seq = n_prompt + c_sample.shape[0]
    pad = (-n_seq) % 8
    c_all = jnp.concatenate([c_prompt, c_sample, jnp.zeros((pad, D_MODEL), F32)], axis=0)
    mod = _modulation(c_all, w_ada, b_ada).reshape(DEPTH, n_seq + pad, N_MOD, D_MODEL)

    perm_q = _deinterleave(A_HEADS)
    perm_k = A_HEADS * HEAD_DIM + _deinterleave(A_KV_HEADS)
    v_cols = jnp.arange((A_HEADS + A_KV_HEADS) * HEAD_DIM, (A_HEADS + 2 * A_KV_HEADS) * HEAD_DIM)
    a_perm = jnp.concatenate([perm_q, perm_k, v_cols])
    head_perm = _deinterleave(1)
    table = rel_bias.astype(F32) * LOG2_E
    p = {
        "g1": norm1_gain.reshape(DEPTH, 1, D_MODEL),
        "g2": norm2_gain.reshape(DEPTH, 1, D_MODEL),
        "a_wt": a_w_qkv[0][:, a_perm].T.astype(BF16),
        "a_gq": _col(a_q_gain[0][head_perm]),
        "a_gk": _col(a_k_gain[0][head_perm]),
        "a_wo": a_w_o[0].astype(BF16),
        "b_wt": b_w_qkv[0].T.astype(BF16),
        "b_gq": _col(b_q_gain[0]),
        "b_gk": _col(b_k_gain[0]),
        "b_wo": b_w_o[0].astype(BF16),
        "lq1": b_lq1.astype(F32), "lk1": b_lk1.astype(F32), "lq2": b_lq2.astype(F32), "lk2": b_lk2.astype(F32),
        "subln": _col(b_subln_gain[0]),
        "table": table,
        "bias_tiles": _bias_tiles(table, _bias_buckets()),
        "w_gu": _chunk_gate_up(w_gate_up),
        "w_d": w_down.astype(BF16),
    }
    y_prompt = _trunk(x_prompt, mod, 0, p, tm=tm)
    y_sample = _trunk(x_sample, mod, n_prompt, p, tm=tm)
    return (y_prompt, y_sample)
```

```python
import functools
import math

import jax
import jax.numpy as jnp
from jax import lax
from jax.experimental import pallas as pl
from jax.experimental.pallas import tpu as pltpu

F32 = jnp.float32
BF16 = jnp.bfloat16

D_MODEL = 1024
DEPTH = 2
GRID_W = 64
HEAD_DIM = 64
A_HEADS = 16
A_KV_HEADS = 4
B_HEADS = 8
B_V_DIM = 128
NUM_BUCKETS = 32
MAX_DISTANCE = 128
ROPE_THETA = 10000.0
D_FF = 2816
N_MOD = 6
EPS = 1e-6

LANES = 128
SUBLANES = 8
BF16_ROWS = 16
KV_CHUNK = 256
ONES_ROWS = BF16_ROWS
Q_BLOCK = 512
CHUNKS_PER_BODY = 8
FF_CHUNK = 256
PRE_ROW_GROUP = 512
DIFF_HEADS_PER_STEP = 2
QK_LOOKAHEAD = 2
NEG_BIG = -1e30
LOG2_E = math.log2(math.e)
SCORE_GUARD = 30.0 * LOG2_E
DENOM_FLOOR = 1e-30
VMEM_LIMIT = 56 * 1024 * 1024


def _cparams(sem):
    return pltpu.CompilerParams(dimension_semantics=sem, vmem_limit_bytes=VMEM_LIMIT)


def _mod_kernel(c_ref, w_ref, b_ref, o_ref):
    c = c_ref[...]
    c_act = (c / (1.0 + jnp.exp(-c))).astype(BF16)
    w = w_ref[0].astype(BF16)
    o_ref[0] = jnp.dot(c_act, w, preferred_element_type=F32) + b_ref[0]


def _modulation(c_all, w_ada, b_ada):
    s = c_all.shape[0]
    tn = 1024
    n_out = N_MOD * D_MODEL
    return pl.pallas_call(
        _mod_kernel,
        grid=(DEPTH, n_out // tn),
        in_specs=[
            pl.BlockSpec((s, D_MODEL), lambda i, n: (0, 0)),
            pl.BlockSpec((1, D_MODEL, tn), lambda i, n: (i, 0, n)),
            pl.BlockSpec((1, 1, tn), lambda i, n: (i, 0, n)),
        ],
        out_specs=pl.BlockSpec((1, s, tn), lambda i, n: (i, 0, n)),
        out_shape=jax.ShapeDtypeStruct((DEPTH, s, n_out), F32),
        compiler_params=_cparams(("arbitrary", "arbitrary")),
        name="adaln_mod",
    )(c_all, w_ada, b_ada.reshape(DEPTH, 1, n_out))


def _rms_modulate(x, gain, scale, shift):
    ms = jnp.mean(x * x, axis=-1, keepdims=True)
    y = x * lax.rsqrt(ms + EPS)
    return (y * gain) * (1.0 + scale) + shift


def _pre_kernel(x_ref, mod_ref, g1_ref, wt_ref, gq_ref, gk_ref, *rest, n_q, n_k, n_v, v_dim, rope):
    if rope:
        cos_ref, sin_ref, q_ref, k_ref, v_ref = rest
    else:
        q_ref, k_ref, v_ref = rest
    tm = x_ref.shape[1]
    mod = mod_ref[0, 0]
    h = _rms_modulate(x_ref[0], g1_ref[0], mod[1:2], mod[0:1]).astype(BF16)
    groups = [lax.dot_general(wt_ref[r:r + PRE_ROW_GROUP], h, (((1,), (1,)), ((), ())), preferred_element_type=F32)
              for r in range(0, wt_ref.shape[0], PRE_ROW_GROUP)]

    def rows(start, size):
        g, off = divmod(start, PRE_ROW_GROUP)
        assert off + size <= PRE_ROW_GROUP
        return groups[g][off:off + size]

    def head_norm(blk, gain):
        ms = jnp.mean(blk * blk, axis=0, keepdims=True)
        y = blk * lax.rsqrt(ms + EPS) * gain
        if rope:
            half = HEAD_DIM // 2
            x1, x2 = y[:half], y[half:]
            c, s = cos_ref[...], sin_ref[...]
            y = jnp.concatenate([x1 * c - x2 * s, x1 * s + x2 * c], axis=0)
        return y

    gq, gk = gq_ref[...], gk_ref[...]
    for i in range(n_q):
        blk = head_norm(rows(i * HEAD_DIM, HEAD_DIM), gq)
        q_ref[0, i * HEAD_DIM:(i + 1) * HEAD_DIM, :] = (blk * (HEAD_DIM ** -0.5 * LOG2_E)).astype(BF16)
    k_rows = n_q * HEAD_DIM
    k_t = jnp.concatenate(
        [head_norm(rows(k_rows + i * HEAD_DIM, HEAD_DIM), gk) for i in range(n_k)], axis=0)
    k_ref[0] = k_t.T.astype(BF16)
    v_rows = k_rows + n_k * HEAD_DIM
    for g in range(n_v):
        v_t = rows(v_rows + g * v_dim, v_dim).astype(BF16)
        for c in range(tm // KV_CHUNK):
            v_ref[0, g, c, 0:v_dim, :] = v_t[:, c * KV_CHUNK:(c + 1) * KV_CHUNK]
            if rope:
                ones_blk = (lax.broadcasted_iota(jnp.int32, (ONES_ROWS, KV_CHUNK), 0) == 0).astype(BF16)
                v_ref[0, g, c, v_dim:v_dim + ONES_ROWS, :] = ones_blk


def _pre_attention(x, mod, layer, seq_off, g1, wt, gq, gk, cos_t, sin_t, *, n_q, n_k, n_v, v_dim, tm):
    b, l, d = x.shape
    rope = cos_t is not None
    n_rows = wt.shape[0]
    nc = l // KV_CHUNK
    in_specs = [
        pl.BlockSpec((1, tm, d), lambda bi, ti: (bi, ti, 0)),
        pl.BlockSpec((1, 1, N_MOD, d), lambda bi, ti: (layer, bi + seq_off, 0, 0)),
        pl.BlockSpec((1, 1, d), lambda bi, ti: (layer, 0, 0)),
        pl.BlockSpec((n_rows, d), lambda bi, ti: (0, 0)),
        pl.BlockSpec((HEAD_DIM, 1), lambda bi, ti: (0, 0)),
        pl.BlockSpec((HEAD_DIM, 1), lambda bi, ti: (0, 0)),
    ]
    args = [x, mod, g1, wt, gq, gk]
    if rope:
        in_specs += [pl.BlockSpec((HEAD_DIM // 2, tm), lambda bi, ti: (0, ti))] * 2
        args += [cos_t, sin_t]
    vr = v_dim + (ONES_ROWS if rope else 0)
    out_shape = (
        jax.ShapeDtypeStruct((b, n_q * HEAD_DIM, l), BF16),
        jax.ShapeDtypeStruct((b, l, n_k * HEAD_DIM), BF16),
        jax.ShapeDtypeStruct((b, n_v, nc, vr, KV_CHUNK), BF16),
    )
    out_specs = (
        pl.BlockSpec((1, n_q * HEAD_DIM, tm), lambda bi, ti: (bi, 0, ti)),
        pl.BlockSpec((1, tm, n_k * HEAD_DIM), lambda bi, ti: (bi, ti, 0)),
        pl.BlockSpec((1, n_v, tm // KV_CHUNK, vr, KV_CHUNK), lambda bi, ti: (bi, 0, ti, 0, 0)),
    )
    kern = functools.partial(_pre_kernel, n_q=n_q, n_k=n_k, n_v=n_v, v_dim=v_dim, rope=rope)
    return pl.pallas_call(
        kern, grid=(b, l // tm), in_specs=in_specs, out_specs=out_specs, out_shape=out_shape,
        compiler_params=_cparams(("parallel", "parallel")),
        name="pre_attn_rope" if rope else "pre_attn_diff",
    )(*args)


def _attend(q_ref, k_ref, v_ref, qpad_scr, m_scr, l_scr, acc_scr, *, n_items, k_tile, k_half, v_index,
            bias_tile, write_out, far=None, l_from_v=False):
    nc = v_ref.shape[2]
    tq = q_ref.shape[2]
    assert nc % CHUNKS_PER_BODY == 0
    n_trips = nc // CHUNKS_PER_BODY
    zeros = jnp.zeros((HEAD_DIM, tq), BF16)
    for i in range(n_items):
        q = q_ref[0, i * HEAD_DIM:(i + 1) * HEAD_DIM, :]
        qpad_scr[i] = jnp.concatenate([q, zeros] if k_half(i) == 0 else [zeros, q], axis=0)

    def scores(j, i, biased=True):
        t = k_tile(i)
        kc = k_ref[0, pl.ds(pl.multiple_of(j * KV_CHUNK, KV_CHUNK), KV_CHUNK), t * LANES:(t + 1) * LANES]
        s = jnp.dot(kc, qpad_scr[i], preferred_element_type=F32)
        b = bias_tile(i, j) if biased else None
        return s if b is None else s + b

    def reset(m_ref, l_ref, acc_ref):
        m_ref[...] = jnp.full(m_ref.shape, NEG_BIG, F32)
        l_ref[...] = jnp.zeros(l_ref.shape, F32)
        acc_ref[...] = jnp.zeros(acc_ref.shape, F32)

    def fast_body(biased, m_ref, l_ref, acc_ref):
        def body(t, carry):
            seq = [(CHUNKS_PER_BODY * t + u, i) for i in range(n_items) for u in range(CHUNKS_PER_BODY)]
            pending = [scores(*seq[n], biased) for n in range(QK_LOOKAHEAD)]
            for n, (j, i) in enumerate(seq):
                s = pending[n]
                if n + QK_LOOKAHEAD < len(seq):
                    pending.append(scores(*seq[n + QK_LOOKAHEAD], biased))
                p = jnp.exp2(s)
                m_tile = jnp.max(s.reshape(-1, SUBLANES, tq), axis=0)
                pv_tile = jnp.dot(v_ref[0, v_index(i), j], p.astype(BF16), preferred_element_type=F32)
                first = n % CHUNKS_PER_BODY == 0
                m_part = m_tile if first else jnp.maximum(m_part, m_tile)
                pv = pv_tile if first else pv + pv_tile
                if not l_from_v:
                    l_tile = jnp.sum(p.reshape(-1, SUBLANES, tq), axis=0)
                    l_part = l_tile if first else l_part + l_tile
                if n % CHUNKS_PER_BODY == CHUNKS_PER_BODY - 1:
                    m_ref[i] = jnp.maximum(m_ref[i], m_part)
                    acc_ref[i] += pv
                    if not l_from_v:
                        l_ref[i] += l_part
            return carry
        return body

    reset(m_scr, l_scr, acc_scr)
    if far is None:
        lax.fori_loop(0, n_trips, fast_body(True, m_scr, l_scr, acc_scr), 0)
    else:
        near_lo, near_hi, side_bias, m_far, l_far, acc_far = far
        reset(m_far, l_far, acc_far)
        lax.fori_loop(0, near_lo, fast_body(False, m_scr, l_scr, acc_scr), 0)
        lax.fori_loop(near_hi, n_trips, fast_body(False, m_far, l_far, acc_far), 0)
        for i in range(n_items):
            c_left, c_right = side_bias(i)
            e_left = jnp.exp2(jnp.full((1, 1), c_left, F32))
            e_right = jnp.exp2(jnp.full((1, 1), c_right, F32))
            acc_scr[i] = e_left * acc_scr[i] + e_right * acc_far[i]
            l_scr[i] = e_left * l_scr[i] + e_right * l_far[i]
            m_scr[i] = jnp.maximum(m_scr[i] + c_left, m_far[i] + c_right)
        lax.fori_loop(near_lo, near_hi, fast_body(True, m_scr, l_scr, acc_scr), 0)
    l_min = write_out(acc_scr, l_scr)
    in_range = jnp.logical_and(jnp.max(m_scr[...]) <= SCORE_GUARD, l_min >= DENOM_FLOOR)

    @pl.when(jnp.logical_not(in_range))
    def _():
        reset(m_scr, l_scr, acc_scr)

        def safe_chunk(j, carry):
            tiles = [scores(j, i) for i in range(n_items)]
            for i in range(n_items):
                s = tiles[i]
                m_old = m_scr[i, 0:1]
                m_new = jnp.maximum(m_old, jnp.max(s, axis=0, keepdims=True))
                alpha = jnp.exp2(m_old - m_new)
                p = jnp.exp2(s - m_new)
                pv = jnp.dot(v_ref[0, v_index(i), j], p.astype(BF16), preferred_element_type=F32)
                acc_scr[i] = alpha * acc_scr[i] + pv
                if not l_from_v:
                    l_scr[i, 0:1] = alpha * l_scr[i, 0:1] + jnp.sum(p, axis=0, keepdims=True)
                m_scr[i, 0:1] = m_new
            return carry

        lax.fori_loop(0, nc, safe_chunk, 0)
        write_out(acc_scr, l_scr)


def _gqa_attn_kernel(q_ref, k_ref, v_ref, o_ref, qpad_scr, m_scr, l_scr, acc_scr, *, n_items, group):
    def write_out(acc_ref, l_ref):
        l_min = jnp.full((1, 1), jnp.inf, F32)
        for i in range(n_items):
            acc = acc_ref[i]
            l = acc[HEAD_DIM:HEAD_DIM + 1]
            l_min = jnp.minimum(l_min, jnp.min(l, axis=1, keepdims=True))
            o_ref[0, i * HEAD_DIM:(i + 1) * HEAD_DIM, :] = (acc[:HEAD_DIM] / l).astype(BF16)
        return l_min[0, 0]

    _attend(q_ref, k_ref, v_ref, qpad_scr, m_scr, l_scr, acc_scr, n_items=n_items,
            k_tile=lambda i: 0, k_half=lambda i: i // group, v_index=lambda i: i // group,
            bias_tile=lambda i, j: None, write_out=write_out, l_from_v=True)


def _gqa_attention(q_t, k, v_t):
    b, _, l = q_t.shape
    tq = Q_BLOCK
    group = A_HEADS // A_KV_HEADS
    n_items = 2 * group
    rows = n_items * HEAD_DIM
    nc = l // KV_CHUNK
    kern = functools.partial(_gqa_attn_kernel, n_items=n_items, group=group)
    return pl.pallas_call(
        kern,
        grid=(b, A_KV_HEADS // 2, l // tq),
        in_specs=[
            pl.BlockSpec((1, rows, tq), lambda bi, p, qi: (bi, p, qi)),
            pl.BlockSpec((1, l, LANES), lambda bi, p, qi: (bi, 0, p)),
            pl.BlockSpec((1, 2, nc, HEAD_DIM + ONES_ROWS, KV_CHUNK), lambda bi, p, qi: (bi, p, 0, 0, 0)),
        ],
        out_specs=pl.BlockSpec((1, rows, tq), lambda bi, p, qi: (bi, p, qi)),
        out_shape=jax.ShapeDtypeStruct(q_t.shape, BF16),
        scratch_shapes=[
            pltpu.VMEM((n_items, 2 * HEAD_DIM, tq), BF16),
            pltpu.VMEM((n_items, SUBLANES, tq), F32),
            pltpu.VMEM((n_items, SUBLANES, tq), F32),
            pltpu.VMEM((n_items, HEAD_DIM + ONES_ROWS, tq), F32),
        ],
        compiler_params=_cparams(("parallel", "parallel", "arbitrary")),
        name="gqa_attention",
    )(q_t, k, v_t)


def _diff_attn_kernel(tbl_ref, q_ref, k_ref, v_ref, bias_ref, lq1_ref, lk1_ref, lq2_ref, lk2_ref, sg_ref,
                      o_ref, qpad_scr, m_scr, l_scr, acc_scr, m_far, l_far, acc_far, *, lambda_init, n_heads):
    hp = pl.program_id(0)
    qi = pl.program_id(2)
    n_tiles = bias_ref.shape[1]
    q_per_k = Q_BLOCK // KV_CHUNK
    n_trips = v_ref.shape[2] // CHUNKS_PER_BODY

    def bias_tile(i, j):
        return bias_ref[i // 2, jnp.clip(j - qi * q_per_k + 2, 0, n_tiles - 1)]

    def side_bias(i):
        h = hp * n_heads + i // 2
        return tbl_ref[NUM_BUCKETS // 2 - 1, h], tbl_ref[NUM_BUCKETS - 1, h]

    near_lo = jnp.maximum(qi * q_per_k - 1, 0) // CHUNKS_PER_BODY
    near_hi = jnp.minimum((qi * q_per_k + q_per_k) // CHUNKS_PER_BODY + 1, n_trips)

    lam = (jnp.exp(jnp.sum(lq1_ref[...] * lk1_ref[...], axis=1, keepdims=True))
           - jnp.exp(jnp.sum(lq2_ref[...] * lk2_ref[...], axis=1, keepdims=True)) + lambda_init)

    def write_out(acc_ref, l_ref):
        l_min = jnp.full((1, 1), jnp.inf, F32)
        for hh in range(n_heads):
            l0 = jnp.sum(l_ref[2 * hh], axis=0, keepdims=True)
            l1 = jnp.sum(l_ref[2 * hh + 1], axis=0, keepdims=True)
            l_min = jnp.minimum(l_min, jnp.min(jnp.minimum(l0, l1), axis=1, keepdims=True))
            o = acc_ref[2 * hh] / l0 - lam * (acc_ref[2 * hh + 1] / l1)
            ms = jnp.mean(o * o, axis=0, keepdims=True)
            y = (o * lax.rsqrt(ms + EPS)) * sg_ref[...]
            o_ref[0, hh * B_V_DIM:(hh + 1) * B_V_DIM, :] = (y * (1.0 - lambda_init)).astype(BF16)
        return l_min[0, 0]

    _attend(q_ref, k_ref, v_ref, qpad_scr, m_scr, l_scr, acc_scr, n_items=2 * n_heads,
            k_tile=lambda i: i // 2, k_half=lambda i: i % 2, v_index=lambda i: i // 2,
            bias_tile=bias_tile, write_out=write_out,
            far=None if n_trips == 1 else (near_lo, near_hi, side_bias, m_far, l_far, acc_far))


def _diff_attention(q_t, k, v_t, bias_tiles, table, lq1, lk1, lq2, lk2, subln_gain, *, lambda_init):
    b, _, l = q_t.shape
    tq = Q_BLOCK
    nh = DIFF_HEADS_PER_STEP
    nc = l // KV_CHUNK
    n_tiles = bias_tiles.shape[1]
    rows = nh * 2 * HEAD_DIM
    kern = functools.partial(_diff_attn_kernel, lambda_init=lambda_init, n_heads=nh)
    vec = pl.BlockSpec((1, HEAD_DIM), lambda hp, bi, qi, *_: (0, 0))
    grid_spec = pltpu.PrefetchScalarGridSpec(
        num_scalar_prefetch=1,
        grid=(B_HEADS // nh, b, l // tq),
        in_specs=[
            pl.BlockSpec((1, rows, tq), lambda hp, bi, qi, *_: (bi, hp, qi)),
            pl.BlockSpec((1, l, nh * LANES), lambda hp, bi, qi, *_: (bi, 0, hp)),
            pl.BlockSpec((1, nh, nc, B_V_DIM, KV_CHUNK), lambda hp, bi, qi, *_: (bi, hp, 0, 0, 0)),
            pl.BlockSpec((nh, n_tiles, KV_CHUNK, tq), lambda hp, bi, qi, *_: (hp, 0, 0, 0),
                         pipeline_mode=pl.Buffered(1)),
            vec, vec, vec, vec,
            pl.BlockSpec((B_V_DIM, 1), lambda hp, bi, qi, *_: (0, 0)),
        ],
        out_specs=pl.BlockSpec((1, rows, tq), lambda hp, bi, qi, *_: (bi, hp, qi)),
        scratch_shapes=[
            pltpu.VMEM((2 * nh, 2 * HEAD_DIM, tq), BF16),
            pltpu.VMEM((2 * nh, SUBLANES, tq), F32),
            pltpu.VMEM((2 * nh, SUBLANES, tq), F32),
            pltpu.VMEM((2 * nh, B_V_DIM, tq), F32),
            pltpu.VMEM((2 * nh, SUBLANES, tq), F32),
            pltpu.VMEM((2 * nh, SUBLANES, tq), F32),
            pltpu.VMEM((2 * nh, B_V_DIM, tq), F32),
        ],
    )
    return pl.pallas_call(
        kern, grid_spec=grid_spec,
        out_shape=jax.ShapeDtypeStruct(q_t.shape, BF16),
        compiler_params=_cparams(("parallel", "parallel", "arbitrary")),
        name="diff_attention",
    )(table, q_t, k, v_t, bias_tiles, lq1, lk1, lq2, lk2, subln_gain)


def _bias_tile_kernel(tbl_ref, bkt_ref, o_ref):
    h = pl.program_id(0)
    bkt = bkt_ref[0]
    acc = jnp.zeros(bkt.shape, F32)
    for b in range(NUM_BUCKETS):
        acc = jnp.where(bkt == b, tbl_ref[b, h], acc)
    o_ref[0, 0] = acc


def _bias_tiles(table, buckets):
    n_t, nk, nq = buckets.shape
    n_h = table.shape[1]
    grid_spec = pltpu.PrefetchScalarGridSpec(
        num_scalar_prefetch=1,
        grid=(n_h, n_t),
        in_specs=[pl.BlockSpec((1, nk, nq), lambda h, t, *_: (t, 0, 0))],
        out_specs=pl.BlockSpec((1, 1, nk, nq), lambda h, t, *_: (h, t, 0, 0)),
    )
    return pl.pallas_call(
        _bias_tile_kernel, grid_spec=grid_spec,
        out_shape=jax.ShapeDtypeStruct((n_h, n_t, nk, nq), F32),
        compiler_params=_cparams(("arbitrary", "arbitrary")),
        name="t5_bias_tiles",
    )(table, buckets)


def _post_kernel(a_ref, wo_ref, x_ref, mod_ref, g2_ref, wgu_ref, wd_ref, o_ref, act_scr):
    mod = mod_ref[0, 0]
    y = lax.dot_general(a_ref[0], wo_ref[...], (((0,), (0,)), ((), ())), preferred_element_type=F32)
    x1 = x_ref[0] + mod[2:3] * y
    h = _rms_modulate(x1, g2_ref[0], mod[4:5], mod[3:4]).astype(BF16)
    for c in range(D_FF // FF_CHUNK):
        gu = jnp.dot(h, wgu_ref[0, c], preferred_element_type=F32)
        g, u = gu[:, :FF_CHUNK], gu[:, FF_CHUNK:]
        act_scr[:, c * FF_CHUNK:(c + 1) * FF_CHUNK] = ((g / (1.0 + jnp.exp(-g))) * u).astype(BF16)
    ffn = jnp.dot(act_scr[...], wd_ref[0], preferred_element_type=F32)
    o_ref[0] = x1 + mod[5:6] * ffn


def _post_attention(attn_t, wo, x, mod, layer, seq_off, g2, w_gu, w_d, *, tm):
    b, l, d = x.shape
    once = pl.Buffered(1)
    return pl.pallas_call(
        _post_kernel,
        grid=(b, l // tm),
        in_specs=[
            pl.BlockSpec((1, d, tm), lambda bi, ti: (bi, 0, ti)),
            pl.BlockSpec((d, d), lambda bi, ti: (0, 0), pipeline_mode=once),
            pl.BlockSpec((1, tm, d), lambda bi, ti: (bi, ti, 0)),
            pl.BlockSpec((1, 1, N_MOD, d), lambda bi, ti: (layer, bi + seq_off, 0, 0)),
            pl.BlockSpec((1, 1, d), lambda bi, ti: (layer, 0, 0)),
            pl.BlockSpec((1, D_FF // FF_CHUNK, d, 2 * FF_CHUNK), lambda bi, ti: (layer, 0, 0, 0), pipeline_mode=once),
            pl.BlockSpec((1, D_FF, d), lambda bi, ti: (layer, 0, 0), pipeline_mode=once),
        ],
        out_specs=pl.BlockSpec((1, tm, d), lambda bi, ti: (bi, ti, 0)),
        out_shape=jax.ShapeDtypeStruct(x.shape, F32),
        scratch_shapes=[pltpu.VMEM((tm, D_FF), BF16)],
        compiler_params=_cparams(("parallel", "parallel")),
        name="post_attn_ffn",
    )(attn_t, wo, x, mod, g2, w_gu, w_d)


def _rope_tables(length):
    n_rows = length // GRID_W
    rr, cc = jnp.meshgrid(jnp.arange(n_rows), jnp.arange(GRID_W), indexing="ij")
    rows = rr.reshape(-1).astype(F32)
    cols = cc.reshape(-1).astype(F32)
    n_pairs = HEAD_DIM // 4
    inv_freq = ROPE_THETA ** (-jnp.arange(n_pairs, dtype=F32) / n_pairs)
    ang = jnp.concatenate([rows[:, None] * inv_freq[None], cols[:, None] * inv_freq[None]], axis=-1)
    return jnp.cos(ang).T, jnp.sin(ang).T


def _t5_bucket(rel):
    nb = NUM_BUCKETS // 2
    max_exact = nb // 2
    ret = (rel > 0).astype(jnp.int32) * nb
    n = jnp.abs(rel)
    large = max_exact + (jnp.log(jnp.maximum(n, 1).astype(F32) / max_exact)
                         / math.log(MAX_DISTANCE / max_exact) * (nb - max_exact)).astype(jnp.int32)
    large = jnp.minimum(large, nb - 1)
    return ret + jnp.where(n < max_exact, n, large)


def _bias_buckets():
    assert Q_BLOCK % KV_CHUNK == 0 and KV_CHUNK >= MAX_DISTANCE
    off = jnp.arange(-2, Q_BLOCK // KV_CHUNK + 2, dtype=jnp.int32)[:, None, None]
    kk = jnp.arange(KV_CHUNK, dtype=jnp.int32)[None, :, None]
    qq = jnp.arange(Q_BLOCK, dtype=jnp.int32)[None, None, :]
    return _t5_bucket(off * KV_CHUNK + kk - qq)


def _deinterleave(n_heads):
    base = jnp.concatenate([jnp.arange(0, HEAD_DIM, 2), jnp.arange(1, HEAD_DIM, 2)])
    return (jnp.arange(n_heads)[:, None] * HEAD_DIM + base[None, :]).reshape(-1)


def _col(v):
    return v.astype(F32).reshape(-1, 1)


def _chunk_gate_up(w_gate_up):
    n = D_FF // FF_CHUNK
    gate = w_gate_up[:, :, :D_FF].reshape(DEPTH, D_MODEL, n, FF_CHUNK)
    up = w_gate_up[:, :, D_FF:].reshape(DEPTH, D_MODEL, n, FF_CHUNK)
    return jnp.transpose(jnp.concatenate([gate, up], axis=-1), (0, 2, 1, 3)).astype(BF16)


def _trunk(x, mod, seq_off, p, *, tm):
    _, l, _ = x.shape
    cos_t, sin_t = _rope_tables(l)
    for layer in range(DEPTH):
        if layer % 2 == 0:
            q_t, k, v_t = _pre_attention(
                x, mod, layer, seq_off, p["g1"], p["a_wt"], p["a_gq"], p["a_gk"], cos_t, sin_t,
                n_q=A_HEADS, n_k=A_KV_HEADS, n_v=A_KV_HEADS, v_dim=HEAD_DIM, tm=tm)
            attn_t = _gqa_attention(q_t, k, v_t)
            wo = p["a_wo"]
        else:
            lambda_init = 0.8 - 0.6 * math.exp(-0.3 * layer)
            q_t, k, v_t = _pre_attention(
                x, mod, layer, seq_off, p["g1"], p["b_wt"], p["b_gq"], p["b_gk"], None, None,
                n_q=2 * B_HEADS, n_k=2 * B_HEADS, n_v=B_HEADS, v_dim=B_V_DIM, tm=tm)
            attn_t = _diff_attention(q_t, k, v_t, p["bias_tiles"], p["table"], p["lq1"], p["lk1"], p["lq2"],
                                     p["lk2"], p["subln"], lambda_init=lambda_init)
            wo = p["b_wo"]
        x = _post_attention(attn_t, wo, x, mod, layer, seq_off, p["g2"], p["w_gu"], p["w_d"], tm=tm)
    return x


def kernel(x_prompt, x_sample, c_prompt, c_sample, norm1_gain, norm2_gain, w_ada, b_ada, w_gate_up, w_down, rel_bias, a_w_qkv, a_w_o, a_q_gain, a_k_gain, b_w_qkv, b_w_o, b_q_gain, b_k_gain, b_lq1, b_lk1, b_lq2, b_lk2, b_subln_gain):
    tm = 512
    n_prompt = c_prompt.shape[0]
    n_seq = n_prompt + c_sample.shape[0]
    pad = (-n_seq) % 8
    c_all = jnp.concatenate([c_prompt, c_sample, jnp.zeros((pad, D_MODEL), F32)], axis=0)
    mod = _modulation(c_all, w_ada, b_ada).reshape(DEPTH, n_seq + pad, N_MOD, D_MODEL)

    perm_q = _deinterleave(A_HEADS)
    perm_k = A_HEADS * HEAD_DIM + _deinterleave(A_KV_HEADS)
    v_cols = jnp.arange((A_HEADS + A_KV_HEADS) * HEAD_DIM, (A_HEADS + 2 * A_KV_HEADS) * HEAD_DIM)
    a_perm = jnp.concatenate([perm_q, perm_k, v_cols])
    head_perm = _deinterleave(1)
    table = rel_bias.astype(F32) * LOG2_E
    p = {
        "g1": norm1_gain.reshape(DEPTH, 1, D_MODEL),
        "g2": norm2_gain.reshape(DEPTH, 1, D_MODEL),
        "a_wt": a_w_qkv[0][:, a_perm].T.astype(BF16),
        "a_gq": _col(a_q_gain[0][head_perm]),
        "a_gk": _col(a_k_gain[0][head_perm]),
        "a_wo": a_w_o[0].astype(BF16),
        "b_wt": b_w_qkv[0].T.astype(BF16),
        "b_gq": _col(b_q_gain[0]),
        "b_gk": _col(b_k_gain[0]),
        "b_wo": b_w_o[0].astype(BF16),
        "lq1": b_lq1.astype(F32), "lk1": b_lk1.astype(F32), "lq2": b_lq2.astype(F32), "lk2": b_lk2.astype(F32),
        "subln": _col(b_subln_gain[0]),
        "table": table,
        "bias_tiles": _bias_tiles(table, _bias_buckets()),
        "w_gu": _chunk_gate_up(w_gate_up),
        "w_d": w_down.astype(BF16),
    }
    y_prompt = _trunk(x_prompt, mod, 0, p, tm=tm)
    y_sample = _trunk(x_sample, mod, n_prompt, p, tm=tm)
    return (y_prompt, y_sample)
```
